```python
import math
import jax, jax.numpy as jnp
from jax import lax
import numpy as np

D_MODEL = 1024
BATCH = 32
SEQ = 256
DEPTH = 4
DEC_BATCH = 2
DEC_SEQ = 4096
PAST_LEN = 256

GRID_W = 64
D_MIX = D_MODEL
D_A = D_MIX // 2
D_B = D_MIX - D_A
H_A = 4
DK = D_A // H_A
DV = D_A // H_A
G_B = 4
C_B = D_B // G_B
CHUNK_MLP = 128
ROWS_PER_CHUNK = CHUNK_MLP // GRID_W
DN_CHUNK = 64
CONV_W = 3
D_FF = 4 * D_MODEL
N_MOD = 6
PROJ = 4 * D_A + 4 * H_A + 2 * D_B
EPS = 1e-6

kernel_name = 'hybrid_deltanet_gmlp_dit_step'


def rmsnorm(x, w):
    xf = x.astype(jnp.float32)
    y = xf * lax.rsqrt(jnp.mean(xf * xf, axis=-1, keepdims=True) + EPS)
    return (y * w).astype(x.dtype)


def layernorm_noaffine(x):
    xf = x.astype(jnp.float32)
    mu = jnp.mean(xf, axis=-1, keepdims=True)
    var = jnp.mean(jnp.square(xf - mu), axis=-1, keepdims=True)
    return (xf - mu) * lax.rsqrt(var + EPS)


def l2norm(x):
    xf = x.astype(jnp.float32)
    return xf * lax.rsqrt(jnp.sum(xf * xf, axis=-1, keepdims=True) + EPS)


def modulation(cond, w_mod_l, b_mod_l):
    m = jax.nn.silu(cond) @ w_mod_l + b_mod_l
    return jnp.split(m[:, None, :], N_MOD, axis=-1)


def centred_conv(x, w):
    pad = CONV_W // 2
    return lax.conv_general_dilated(x, w[:, None, :], window_strides=(1,), padding=[(pad, pad)],
                                    dimension_numbers=('NWC', 'WIO', 'NWC'),
                                    feature_group_count=x.shape[-1])


def gated_delta_chunked(q, k, v, g, beta, s0):
    out_dtype = v.dtype
    B, T, H, _ = q.shape
    n = T // DN_CHUNK
    f32 = jnp.float32

    def to_chunks(a):
        a = a.astype(f32).reshape((B, n, DN_CHUNK, H) + a.shape[3:])
        return jnp.swapaxes(a, 2, 3)

    q = to_chunks(q) * (DK ** -0.5)
    k = to_chunks(k)
    v = to_chunks(v)
    g = jnp.cumsum(to_chunks(g), axis=-1)
    beta = to_chunks(beta)[..., None]
    k_beta = k * beta
    incl = jnp.tril(jnp.ones((DN_CHUNK, DN_CHUNK), dtype=bool))
    strict = jnp.tril(jnp.ones((DN_CHUNK, DN_CHUNK), dtype=bool), -1)
    diff = g[..., :, None] - g[..., None, :]
    decay = jnp.where(incl, jnp.exp(jnp.where(incl, diff, 0.0)), 0.0)
    a_mat = jnp.where(strict, jnp.einsum('bnhcd,bnhsd->bnhcs', k_beta, k) * decay, 0.0)
    rhs = jnp.concatenate([v * beta, k_beta * jnp.exp(g)[..., None]], axis=-1)
    sol = lax.linalg.triangular_solve(a_mat + jnp.eye(DN_CHUNK, dtype=f32), rhs,
                                      left_side=True, lower=True, unit_diagonal=True)
    u, w = sol[..., :DV], sol[..., DV:]
    qk = jnp.einsum('bnhcd,bnhsd->bnhcs', q, k) * decay
    q_dec = q * jnp.exp(g)[..., None]
    g_last = g[..., -1]
    k_dec = k * jnp.exp(g_last[..., None] - g)[..., None]

    def step(S, xs):
        qd, kd, u_c, w_c, qk_c, gl = xs
        v_new = u_c - jnp.einsum('bhck,bhkv->bhcv', w_c, S)
        o = jnp.einsum('bhck,bhkv->bhcv', qd, S) + jnp.einsum('bhcs,bhsv->bhcv', qk_c, v_new)
        S = S * jnp.exp(gl)[..., None, None] + jnp.einsum('bhck,bhcv->bhkv', kd, v_new)
        return S, o

    xs = tuple(jnp.moveaxis(a, 1, 0) for a in (q_dec, k_dec, u, w, qk, g_last))
    s_fin, o = lax.scan(step, s0.astype(f32), xs)
    o = jnp.transpose(o, (1, 0, 3, 2, 4)).reshape(B, T, H, DV)
    return o.astype(out_dtype), s_fin.astype(s0.dtype)


def delta_bidir(q, k, v, g2, beta2, s0):
    o_f, s_f = gated_delta_chunked(q, k, v, g2[:, :, 0], beta2[:, :, 0], s0[:, 0])
    flip = lambda a: a[:, ::-1]
    o_b, s_b = gated_delta_chunked(flip(q), flip(k), flip(v), flip(g2[:, :, 1]),
                                   flip(beta2[:, :, 1]), s0[:, 1])
    return o_f + flip(o_b), jnp.stack([s_f, s_b], axis=1)


def mixer(h, s0, n_chunks, w_in_l, conv_l, a_log_l, dt_bias_l, o_norm_l, ws_l, bs_l, w_out_l):
    B, T, _ = h.shape
    proj = h @ w_in_l
    cuts = [3 * D_A, 4 * D_A, 4 * D_A + 4 * H_A, 4 * D_A + 4 * H_A + D_B]
    qkv, gate, ab, u, vg = jnp.split(proj, cuts, axis=-1)
    qkv = jax.nn.silu(centred_conv(qkv, conv_l))
    q, k, v = jnp.split(qkv, 3, axis=-1)
    q = l2norm(q.reshape(B, T, H_A, DK))
    k = l2norm(k.reshape(B, T, H_A, DK))
    v = v.reshape(B, T, H_A, DV)
    ab = ab.reshape(B, T, 2, 2, H_A)
    beta2 = jax.nn.sigmoid(ab[:, :, 0].astype(jnp.float32))
    g2 = -jnp.exp(a_log_l.astype(jnp.float32)) * jax.nn.softplus(
        ab[:, :, 1].astype(jnp.float32) + dt_bias_l.astype(jnp.float32))
    o_a, s_new = delta_bidir(q, k, v, g2, beta2, s0)
    o_a = rmsnorm(o_a, o_norm_l) * jax.nn.silu(gate.reshape(B, T, H_A, DV))
    u = jax.nn.gelu(u)
    vg = layernorm_noaffine(jax.nn.gelu(vg).reshape(B, n_chunks, CHUNK_MLP, G_B, C_B))
    mixed = jnp.einsum('gts,bnsgc->bntgc', ws_l, vg) + bs_l.T[None, None, :, :, None]
    o_b = u * mixed.reshape(B, T, D_B).astype(u.dtype)
    out = jnp.concatenate([o_a.reshape(B, T, D_A).astype(h.dtype), o_b.astype(h.dtype)], axis=-1)
    return out @ w_out_l, s_new


def block(x, s0, mod, n_chunks, norm_mix_l, w_in_l, conv_l, a_log_l, dt_bias_l, o_norm_l,
          ws_l, bs_l, w_out_l, norm_ffn_l, w_ff1_l, w_ff2_l):
    sh1, sc1, ga1, sh2, sc2, ga2 = mod
    h = rmsnorm(x, norm_mix_l) * (1.0 + sc1) + sh1
    o, s_new = mixer(h, s0, n_chunks, w_in_l, conv_l, a_log_l, dt_bias_l, o_norm_l, ws_l, bs_l, w_out_l)
    x = x + ga1 * o
    h = rmsnorm(x, norm_ffn_l) * (1.0 + sc2) + sh2
    x = x + ga2 * (jnp.square(jax.nn.relu(h @ w_ff1_l)) @ w_ff2_l)
    return x, s_new


def setup_inputs(seed: int = 0) -> dict:
    key = jax.random.key(seed)
    ks = jax.random.split(key, 24)
    f32 = jnp.float32
    nrm = lambda k, shape, s: jax.random.normal(k, shape, f32) * s
    x_prompt = nrm(ks[0], (BATCH, SEQ, D_MODEL), 1.0)
    x_sample = nrm(ks[1], (DEC_BATCH, DEC_SEQ, D_MODEL), 1.0)
    state_delta = nrm(ks[2], (DEC_BATCH, DEPTH, 2, H_A, DK, DV), 0.2)
    c = nrm(ks[3], (DEC_BATCH, D_MODEL), 1.0)
    c_ctx = nrm(ks[4], (D_MODEL,), 1.0)
    w_mod = nrm(ks[5], (DEPTH, D_MODEL, N_MOD * D_MODEL), 0.5 * D_MODEL ** -0.5)
    b_mod = nrm(ks[6], (DEPTH, N_MOD * D_MODEL), 0.02)
    norm_mix = 1.0 + nrm(ks[7], (DEPTH, D_MODEL), 0.02)
    w_in = nrm(ks[8], (DEPTH, D_MODEL, PROJ), D_MODEL ** -0.5)
    conv_qkv = nrm(ks[9], (DEPTH, CONV_W, 3 * D_A), CONV_W ** -0.5)
    a_log = jnp.log(jax.random.uniform(ks[10], (DEPTH, 2, H_A), f32, 1.0, 16.0))
    dt = jnp.exp(jax.random.uniform(ks[11], (DEPTH, 2, H_A), f32, math.log(1e-3), math.log(1e-1)))
    dt_bias = dt + jnp.log(-jnp.expm1(-dt))
    o_norm = 1.0 + nrm(ks[12], (DEPTH, DV), 0.02)
    w_spatial = nrm(ks[13], (DEPTH, G_B, CHUNK_MLP, CHUNK_MLP), CHUNK_MLP ** -0.5)
    b_spatial = 1.0 + nrm(ks[14], (DEPTH, G_B, CHUNK_MLP), 0.02)
    w_out = nrm(ks[15], (DEPTH, D_MIX, D_MODEL), D_MIX ** -0.5)
    norm_ffn = 1.0 + nrm(ks[16], (DEPTH, D_MODEL), 0.02)
    w_ff1 = nrm(ks[17], (DEPTH, D_MODEL, D_FF), D_MODEL ** -0.5)
    w_ff2 = nrm(ks[18], (DEPTH, D_FF, D_MODEL), D_FF ** -0.5)
    norm_final = 1.0 + nrm(ks[19], (D_MODEL,), 0.02)
    return {'x_prompt': x_prompt, 'x_sample': x_sample, 'state_delta': state_delta,
            'c': c, 'c_ctx': c_ctx, 'w_mod': w_mod, 'b_mod': b_mod, 'norm_mix': norm_mix,
            'w_in': w_in, 'conv_qkv': conv_qkv, 'a_log': a_log, 'dt_bias': dt_bias,
            'o_norm': o_norm, 'w_spatial': w_spatial, 'b_spatial': b_spatial, 'w_out': w_out,
            'norm_ffn': norm_ffn, 'w_ff1': w_ff1, 'w_ff2': w_ff2, 'norm_final': norm_final}


def reference(x_prompt, x_sample, state_delta, c, c_ctx, w_mod, b_mod, norm_mix, w_in, conv_qkv,
              a_log, dt_bias, o_norm, w_spatial, b_spatial, w_out, norm_ffn, w_ff1, w_ff2,
              norm_final):
    x = x_prompt
    n_ctx_chunks = x.shape[1] // CHUNK_MLP
    zero_state = jnp.zeros((x.shape[0], 2, H_A, DK, DV), x.dtype)
    y = x_sample
    rows = y.shape[1] // GRID_W
    n_lat_chunks = rows // ROWS_PER_CHUNK
    new_states = []
    for l in range(DEPTH):
        p = (norm_mix[l], w_in[l], conv_qkv[l], a_log[l], dt_bias[l], o_norm[l],
             w_spatial[l], b_spatial[l], w_out[l], norm_ffn[l], w_ff1[l], w_ff2[l])
        mod_ctx = modulation(c_ctx[None, :], w_mod[l], b_mod[l])
        x, s_ctx = block(x, zero_state, mod_ctx, n_ctx_chunks, *p)
        new_states.append(s_ctx)
        mod_lat = modulation(c, w_mod[l], b_mod[l])
        y, _ = block(y, state_delta[:, l], mod_lat, n_lat_chunks, *p)
    y_prompt = rmsnorm(x, norm_final)
    y_sample = rmsnorm(y, norm_final)
    new_state_delta = jnp.stack(new_states, axis=1)
    return (y_prompt, y_sample, new_state_delta)
```

```python
import functools

import jax
import jax.numpy as jnp
from jax import lax
from jax.experimental import pallas as pl
from jax.experimental.pallas import tpu as pltpu

F32 = jnp.float32
BF16 = jnp.bfloat16

D_MODEL = 1024
DEPTH = 4
N_MOD = 6
H_A = 4
DK = 128
DV = 128
D_A = H_A * DK
G_B = 4
C_B = 128
D_B = G_B * C_B
CHUNK_MLP = 128
D_FF = 4 * D_MODEL
EPS = 1e-6

CTX_B, CTX_T = 32, 256
LAT_B, LAT_T = 2, 4096
N_CTX = CTX_B * CTX_T
N_LAT = LAT_B * LAT_T
N_TOK = N_CTX + N_LAT

TM = 512
N_TILES = N_TOK // TM
CTX_TILES = N_CTX // TM
LAT_TILES_PER_SEQ = LAT_T // TM
HALO = 8

CHUNK = 128
INV_BASE = 8
TD = 256
CHUNKS_PER_TD = TD // CHUNK
CTX_DTILES = N_CTX // TD
LAT_DTILES_PER_SEQ = LAT_T // TD
N_DTILES = N_TOK // TD

N_FEAT = 6 * D_A
VMEM_LIMIT_BYTES = 60000 * 1024


def _sigmoid(x):
    return 1.0 / (1.0 + jnp.exp(-x))


def _gelu_tanh(x):
    c = 0.7978845608028654
    return x * (0.5 * (1.0 + jnp.tanh(c * (x + 0.044715 * (x * x * x)))))


def _softplus(x):
    return jnp.maximum(x, 0.0) + jnp.log1p(jnp.exp(-jnp.abs(x)))


def _dot(a, b):
    return jnp.dot(a, b, preferred_element_type=F32)


def _dot_nt(a, b):
    return lax.dot_general(a, b, (((1,), (1,)), ((), ())), preferred_element_type=F32)


def _dot_tn(a, b):
    return lax.dot_general(a, b, (((0,), (0,)), ((), ())), preferred_element_type=F32)


def _dot_f32(a, b):
    return jnp.dot(a, b, preferred_element_type=F32, precision=lax.Precision.HIGHEST)


def _mod_kernel(cond_ref, w_ref, b_ref, o_ref):
    c = cond_ref[...]
    s = (c * _sigmoid(c)).astype(BF16)
    o_ref[0] = _dot(s, w_ref[0].astype(BF16)) + b_ref[0]


def _modulation(cond8, w_mod, b_mod):
    tn = 1536
    return pl.pallas_call(
        _mod_kernel,
        grid=(DEPTH, (N_MOD * D_MODEL) // tn),
        in_specs=[
            pl.BlockSpec((8, D_MODEL), lambda l, j: (0, 0)),
            pl.BlockSpec((1, D_MODEL, tn), lambda l, j: (l, 0, j)),
            pl.BlockSpec((1, 1, tn), lambda l, j: (l, 0, j)),
        ],
        out_specs=pl.BlockSpec((1, 8, tn), lambda l, j: (l, 0, j)),
        out_shape=jax.ShapeDtypeStruct((DEPTH, 8, N_MOD * D_MODEL), F32),
        compiler_params=pltpu.CompilerParams(
            dimension_semantics=("arbitrary", "arbitrary"), vmem_limit_bytes=VMEM_LIMIT_BYTES),
        name="modulation",
    )(cond8, w_mod, b_mod.reshape(DEPTH, 1, N_MOD * D_MODEL))


def _mod_row(i):
    return jnp.where(i < CTX_TILES, 0, 1 + (i - CTX_TILES) // LAT_TILES_PER_SEQ)


def _pre_kernel(x_ref, xp_ref, xn_ref, mod_ref, nw_ref, w_ref, wab_ref, conv_ref, arow_ref,
                dtrow_ref, feat_ref, gb_ref):
    i = pl.program_id(0)
    m = mod_ref[0]
    sh1 = m[:, 0:D_MODEL]
    sc1 = m[:, D_MODEL:2 * D_MODEL]
    nw = nw_ref[...]

    def norm_mod(x):
        ms = jnp.mean(x * x, axis=-1, keepdims=True)
        y = x * lax.rsqrt(ms + EPS) * nw
        return (y * (1.0 + sc1) + sh1).astype(BF16)

    h = norm_mod(x_ref[...])
    hh = norm_mod(jnp.concatenate([xp_ref[...], xn_ref[...]], axis=0))
    h_ext = jnp.concatenate([h, hh], axis=0)

    r = lax.broadcasted_iota(jnp.int32, (TM, 1), 0)
    len_mask = jnp.where(i < CTX_TILES, CTX_T - 1, LAT_T - 1)
    pos = (i * TM + r) & len_mask
    is_start = pos == 0
    is_end = pos == len_mask

    cw = conv_ref[...]
    for blk in range(3):
        cols = slice(blk * D_A, (blk + 1) * D_A)
        pe = _dot(h_ext, w_ref[:, cols])
        p = pe[0:TM]
        prev_row = pe[TM + HALO - 1:TM + HALO]
        next_row = pe[TM + HALO:TM + HALO + 1]
        xm1 = jnp.where(r == 0, prev_row, pltpu.roll(p, 1, 0))
        xm1 = jnp.where(is_start, 0.0, xm1)
        xp1 = jnp.where(r == TM - 1, next_row, pltpu.roll(p, TM - 1, 0))
        xp1 = jnp.where(is_end, 0.0, xp1)
        y = cw[0:1, cols] * xm1 + cw[1:2, cols] * p + cw[2:3, cols] * xp1
        y = y * _sigmoid(y)
        if blk < 2:
            for hd in range(H_A):
                yh = y[:, hd * DK:(hd + 1) * DK]
                yh = yh * lax.rsqrt(jnp.sum(yh * yh, axis=-1, keepdims=True) + EPS)
                if blk == 0:
                    yh = yh * (DK ** -0.5)
                feat_ref[:, blk * D_A + hd * DK:blk * D_A + (hd + 1) * DK] = yh.astype(BF16)
        else:
            feat_ref[:, cols] = y.astype(BF16)

    gate = _dot(h, w_ref[:, 3 * D_A:4 * D_A])
    feat_ref[:, 3 * D_A:4 * D_A] = (gate * _sigmoid(gate)).astype(BF16)
    u = _dot(h, w_ref[:, 4 * D_A:5 * D_A])
    feat_ref[:, 4 * D_A:5 * D_A] = _gelu_tanh(u).astype(BF16)
    vg = _gelu_tanh(_dot(h, w_ref[:, 5 * D_A:6 * D_A]))
    for g in range(G_B):
        vgg = vg[:, g * C_B:(g + 1) * C_B]
        mu = jnp.mean(vgg, axis=-1, keepdims=True)
        dv = vgg - mu
        var = jnp.mean(dv * dv, axis=-1, keepdims=True)
        feat_ref[:, 5 * D_A + g * C_B:5 * D_A + (g + 1) * C_B] = (dv * lax.rsqrt(var + EPS)).astype(BF16)

    ab = _dot(h, wab_ref[...])
    lane = lax.broadcasted_iota(jnp.int32, (TM, 128), 1)
    beta = _sigmoid(ab)
    g_log = -jnp.exp(arow_ref[...]) * _softplus(ab + dtrow_ref[...])
    gb_ref[...] = jnp.where(lane < 2 * H_A, beta, g_log)


def _pre_call(layer, x_all, mod, norm_mix, w_main, w_ab, conv_qkv, a_row, dt_row):
    blocks_per_tile = TM // HALO
    n_halo_blocks = N_TOK // HALO
    const = dict(pipeline_mode=pl.Buffered(1))
    return pl.pallas_call(
        _pre_kernel,
        grid=(N_TILES,),
        in_specs=[
            pl.BlockSpec((TM, D_MODEL), lambda i: (i, 0)),
            pl.BlockSpec((HALO, D_MODEL), lambda i: (jnp.maximum(i * blocks_per_tile - 1, 0), 0)),
            pl.BlockSpec((HALO, D_MODEL),
                         lambda i: (jnp.minimum((i + 1) * blocks_per_tile, n_halo_blocks - 1), 0)),
            pl.BlockSpec((None, 1, 1, N_MOD * D_MODEL), lambda i: (layer, _mod_row(i), 0, 0)),
            pl.BlockSpec((None, 1, D_MODEL), lambda i: (layer, 0, 0), **const),
            pl.BlockSpec((None, D_MODEL, N_FEAT), lambda i: (layer, 0, 0), **const),
            pl.BlockSpec((None, D_MODEL, 128), lambda i: (layer, 0, 0), **const),
            pl.BlockSpec((None, 3, 3 * D_A), lambda i: (layer, 0, 0), **const),
            pl.BlockSpec((None, 1, 128), lambda i: (layer, 0, 0), **const),
            pl.BlockSpec((None, 1, 128), lambda i: (layer, 0, 0), **const),
        ],
        out_specs=[
            pl.BlockSpec((TM, N_FEAT), lambda i: (i, 0)),
            pl.BlockSpec((TM, 128), lambda i: (i, 0)),
        ],
        out_shape=[
            jax.ShapeDtypeStruct((N_TOK, N_FEAT), BF16),
            jax.ShapeDtypeStruct((N_TOK, 128), F32),
        ],
        compiler_params=pltpu.CompilerParams(
            dimension_semantics=("arbitrary",), vmem_limit_bytes=VMEM_LIMIT_BYTES),
        name=f"pre_l{layer}",
    )(x_all, x_all, x_all, mod, norm_mix, w_main, w_ab, conv_qkv, a_row, dt_row)


def _delta_tile_row(d, t):
    lat = t - CTX_DTILES
    b = lat // LAT_DTILES_PER_SEQ
    j = lat % LAT_DTILES_PER_SEQ
    j = jnp.where(d == 1, LAT_DTILES_PER_SEQ - 1 - j, j)
    return jnp.where(t < CTX_DTILES, t, CTX_DTILES + b * LAT_DTILES_PER_SEQ + j)


def _delta_kernel(q_ref, k_ref, v_ref, gb_ref, s0_ref, ns_in_ref, o_ref, ns_ref, s_scr):
    del ns_in_ref
    d = pl.program_id(0)
    t = pl.program_id(1)
    rev = d == 1
    is_ctx = t < CTX_DTILES

    @pl.when(is_ctx)
    def _():
        s_scr[...] = jnp.zeros_like(s_scr)

    @pl.when(jnp.logical_and(t >= CTX_DTILES, (t - CTX_DTILES) % LAT_DTILES_PER_SEQ == 0))
    def _():
        s_scr[...] = s0_ref[...]

    row = lax.broadcasted_iota(jnp.int32, (CHUNK, CHUNK), 0)
    col = lax.broadcasted_iota(jnp.int32, (CHUNK, CHUNK), 1)
    sgn = jnp.where(rev, -1, 1)
    incl = (row - col) * sgn >= 0
    strict = (row - col) * sgn > 0
    eye = (row == col).astype(F32)
    base_mask = (row // INV_BASE) == (col // INV_BASE)
    off_masks = []
    blk = INV_BASE
    while blk < CHUNK:
        off_masks.append(jnp.logical_and((row // (2 * blk)) == (col // (2 * blk)),
                                         (row // blk) != (col // blk)))
        blk *= 2
    rix =lax.broadcasted_iota(jnp.int32, (CHUNK, 128), 0)

    for c in range(CHUNKS_PER_TD):
        cc = jnp.where(rev, CHUNKS_PER_TD - 1 - c, c)
        rows = pl.ds(pl.multiple_of(cc * CHUNK, CHUNK), CHUNK)
        gb = gb_ref[rows, :]
        fwd_sum = gb
        s = 1
        while s < CHUNK:
            fwd_sum = fwd_sum + jnp.where(rix >= s, pltpu.roll(fwd_sum, s, 0), 0.0)
            s *= 2
        total = fwd_sum[CHUNK - 1:CHUNK, :]
        cum = jnp.where(rev, total - fwd_sum + gb, fwd_sum)
        for hd in range(H_A):
            beta = jnp.where(rev, gb[:, H_A + hd:H_A + hd + 1], gb[:, hd:hd + 1])
            g_c = jnp.where(rev, cum[:, 3 * H_A + hd:3 * H_A + hd + 1],
                            cum[:, 2 * H_A + hd:2 * H_A + hd + 1])
            g_tot = jnp.where(rev, total[:, 3 * H_A + hd:3 * H_A + hd + 1],
                              total[:, 2 * H_A + hd:2 * H_A + hd + 1])
            g_b = jnp.broadcast_to(g_c, (CHUNK, CHUNK))
            diff = g_b - g_b.T
            decay = jnp.where(incl, jnp.exp(jnp.where(incl, diff, 0.0)), 0.0)
            eg = jnp.exp(g_c)

            hcols = slice(hd * DK, (hd + 1) * DK)
            q_bf = q_ref[rows, hcols]
            k_bf = k_ref[rows, hcols]
            k = k_bf.astype(F32)
            q = q_bf.astype(F32)
            v = v_ref[rows, hcols].astype(F32)
            kb = k * beta
            a = jnp.where(strict, _dot_nt(kb.astype(BF16), k_bf) * decay, 0.0)
            a0 = jnp.where(base_mask, a, 0.0)
            inv = eye - a0
            pw = a0
            n = 2
            while n < INV_BASE:
                pw = _dot_f32(pw, pw)
                inv = inv + _dot_f32(inv, pw)
                n *= 2
            for off_mask in off_masks:
                inv = inv - _dot_f32(_dot_f32(inv, jnp.where(off_mask, a, 0.0)), inv)
            rhs = jnp.concatenate([v * beta, kb * eg], axis=-1)
            sol = _dot_f32(inv, rhs)
            u = sol[:, 0:DV]
            w = sol[:, DV:2 * DV]
            qk = jnp.where(incl, _dot_nt(q_bf, k_bf) * decay, 0.0)
            q_dec = (q * eg).astype(BF16)
            k_dec = (k * jnp.exp(g_tot - g_c)).astype(BF16)

            st = s_scr[hd]
            st_bf = st.astype(BF16)
            v_new = u - _dot(w.astype(BF16), st_bf)
            o = _dot(q_dec, st_bf) + _dot(qk.astype(BF16), v_new.astype(BF16))
            s_scr[hd] = st * jnp.exp(g_tot) + _dot_tn(k_dec, v_new.astype(BF16))
            o_ref[rows, hcols] = o

    @pl.when(is_ctx)
    def _():
        ns_ref[...] = s_scr[...]


def _delta_call(layer, feat, gb, state_delta, new_state):
    def feat_spec(col_block):
        return pl.BlockSpec((TD, D_A), lambda d, t: (_delta_tile_row(d, t), col_block))

    def s0_index(d, t):
        b = jnp.maximum(t - CTX_DTILES, 0) // LAT_DTILES_PER_SEQ
        return (b, layer, d, 0, 0, 0)

    def ns_index(d, t):
        return (jnp.minimum(t, CTX_DTILES - 1), layer, d, 0, 0, 0)

    return pl.pallas_call(
        _delta_kernel,
        grid=(2, N_DTILES),
        in_specs=[
            feat_spec(0), feat_spec(1), feat_spec(2),
            pl.BlockSpec((TD, 128), lambda d, t: (_delta_tile_row(d, t), 0)),
            pl.BlockSpec((None, None, None, H_A, DK, DV), s0_index),
            pl.BlockSpec(memory_space=pl.ANY),
        ],
        out_specs=[
            pl.BlockSpec((None, TD, D_A), lambda d, t: (d, _delta_tile_row(d, t), 0)),
            pl.BlockSpec((None, None, None, H_A, DK, DV), ns_index),
        ],
        out_shape=[
            jax.ShapeDtypeStruct((2, N_TOK, D_A), F32),
            jax.ShapeDtypeStruct(new_state.shape, F32),
        ],
        scratch_shapes=[pltpu.VMEM((H_A, DK, DV), F32)],
        input_output_aliases={5: 1},
        compiler_params=pltpu.CompilerParams(
            dimension_semantics=("arbitrary", "arbitrary"), vmem_limit_bytes=VMEM_LIMIT_BYTES),
        name=f"delta_l{layer}",
    )(feat, feat, feat, gb, state_delta, new_state)


def _post_kernel(final, x_ref, of_ref, ob_ref, feat_ref, mod_ref, onorm_ref, ws_ref, bs_ref,
                 wout_ref, nffn_ref, wff1_ref, wff2_ref, nfin_ref, y_ref):
    m = mod_ref[0]
    ga1 = m[:, 2 * D_MODEL:3 * D_MODEL]
    sh2 = m[:, 3 * D_MODEL:4 * D_MODEL]
    sc2 = m[:, 4 * D_MODEL:5 * D_MODEL]
    ga2 = m[:, 5 * D_MODEL:6 * D_MODEL]

    o = of_ref[...] + ob_ref[...]
    onw = onorm_ref[...]
    mix_in = []
    for hd in range(H_A):
        oh = o[:, hd * DV:(hd + 1) * DV]
        oh = oh * lax.rsqrt(jnp.mean(oh * oh, axis=-1, keepdims=True) + EPS) * onw
        gate = feat_ref[:, hd * DV:(hd + 1) * DV].astype(F32)
        mix_in.append((oh * gate).astype(BF16))
    bs = bs_ref[...]
    for g in range(G_B):
        parts = []
        for c in range(TM // CHUNK_MLP):
            rows = slice(c * CHUNK_MLP, (c + 1) * CHUNK_MLP)
            vgn = feat_ref[rows, 2 * D_A + g * C_B:2 * D_A + (g + 1) * C_B]
            mixed = _dot(ws_ref[g], vgn) + bs[:, g:g + 1]
            ug = feat_ref[rows, D_A + g * C_B:D_A + (g + 1) * C_B].astype(F32)
            parts.append((ug * mixed).astype(BF16))
        mix_in.append(jnp.concatenate(parts, axis=0))
    mix = jnp.concatenate(mix_in, axis=-1)
    x1 = x_ref[...] + ga1 * _dot(mix, wout_ref[...])

    ms = jnp.mean(x1 * x1, axis=-1, keepdims=True)
    h2 = (x1 * lax.rsqrt(ms + EPS) * nffn_ref[...] * (1.0 + sc2) + sh2).astype(BF16)
    a = jnp.maximum(_dot(h2, wff1_ref[...]), 0.0)
    x2 = x1 + ga2 * _dot((a * a).astype(BF16), wff2_ref[...])
    if final:
        ms2 = jnp.mean(x2 * x2, axis=-1, keepdims=True)
        x2 = x2 * lax.rsqrt(ms2 + EPS) * nfin_ref[...]
    y_ref[...] = x2


def _post_call(layer, final, x_all, o_dir, feat, mod, o_norm, ws, bs_t, w_out, norm_ffn, w_ff1,
               w_ff2, norm_final):
    const = dict(pipeline_mode=pl.Buffered(1))
    return pl.pallas_call(
        functools.partial(_post_kernel, final),
        grid=(N_TILES,),
        in_specs=[
            pl.BlockSpec((TM, D_MODEL), lambda i: (i, 0)),
            pl.BlockSpec((None, TM, D_A), lambda i: (0, i, 0)),
            pl.BlockSpec((None, TM, D_A), lambda i: (1, i, 0)),
            pl.BlockSpec((TM, 3 * D_A), lambda i: (i, 1)),
            pl.BlockSpec((None, 1, 1, N_MOD * D_MODEL), lambda i: (layer, _mod_row(i), 0, 0)),
            pl.BlockSpec((None, 1, DV), lambda i: (layer, 0, 0), **const),
            pl.BlockSpec((None, G_B, CHUNK_MLP, CHUNK_MLP), lambda i: (layer, 0, 0, 0), **const),
            pl.BlockSpec((None, CHUNK_MLP, 128), lambda i: (layer, 0, 0), **const),
            pl.BlockSpec((None, D_MODEL, D_MODEL), lambda i: (layer, 0, 0), **const),
            pl.BlockSpec((None, 1, D_MODEL), lambda i: (layer, 0, 0), **const),
            pl.BlockSpec((None, D_MODEL, D_FF), lambda i: (layer, 0, 0), **const),
            pl.BlockSpec((None, D_FF, D_MODEL), lambda i: (layer, 0, 0), **const),
            pl.BlockSpec((1, D_MODEL), lambda i: (0, 0), **const),
        ],
        out_specs=pl.BlockSpec((TM, D_MODEL), lambda i: (i, 0)),
        out_shape=jax.ShapeDtypeStruct((N_TOK, D_MODEL), F32),
        compiler_params=pltpu.CompilerParams(
            dimension_semantics=("arbitrary",), vmem_limit_bytes=VMEM_LIMIT_BYTES),
        name=f"post_l{layer}",
    )(x_all, o_dir, o_dir, feat, mod, o_norm, ws, bs_t, w_out, norm_ffn, w_ff1, w_ff2, norm_final)


def kernel(x_prompt, x_sample, state_delta, c, c_ctx, w_mod, b_mod, norm_mix, w_in, conv_qkv, a_log,
           dt_bias, o_norm, w_spatial, b_spatial, w_out, norm_ffn, w_ff1, w_ff2, norm_final):
    x_all = jnp.concatenate(
        [x_prompt.reshape(N_CTX, D_MODEL), x_sample.reshape(N_LAT, D_MODEL)], axis=0)
    cond8 = jnp.concatenate([c_ctx[None, :], c, jnp.zeros((8 - 1 - LAT_B, D_MODEL), F32)], axis=0)

    n_qkvg = 4 * D_A
    w_main = jnp.concatenate([w_in[:, :, :n_qkvg], w_in[:, :, n_qkvg + 4 * H_A:]], axis=-1).astype(BF16)
    w_ab = jnp.pad(w_in[:, :, n_qkvg:n_qkvg + 4 * H_A], ((0, 0), (0, 0), (0, 128 - 4 * H_A))).astype(BF16)
    a_row = jnp.pad(a_log.reshape(DEPTH, 1, 2 * H_A), ((0, 0), (0, 0), (2 * H_A, 128 - 4 * H_A)))
    dt_row = jnp.pad(dt_bias.reshape(DEPTH, 1, 2 * H_A), ((0, 0), (0, 0), (2 * H_A, 128 - 4 * H_A)))
    bs_t = jnp.pad(jnp.swapaxes(b_spatial, 1, 2), ((0, 0), (0, 0), (0, 128 - G_B)))
    ws_bf = w_spatial.astype(BF16)
    w_out_bf = w_out.astype(BF16)
    w_ff1_bf = w_ff1.astype(BF16)
    w_ff2_bf = w_ff2.astype(BF16)
    norm_mix3 = norm_mix.reshape(DEPTH, 1, D_MODEL)
    norm_ffn3 = norm_ffn.reshape(DEPTH, 1, D_MODEL)
    o_norm3 = o_norm.reshape(DEPTH, 1, DV)
    norm_final2 = norm_final.reshape(1, D_MODEL)

    mod = _modulation(cond8, w_mod, b_mod).reshape(DEPTH, 8, 1, N_MOD * D_MODEL)

    new_state = jnp.zeros((CTX_B, DEPTH, 2, H_A, DK, DV), F32)
    for layer in range(DEPTH):
        feat, gb = _pre_call(layer, x_all, mod, norm_mix3, w_main, w_ab, conv_qkv, a_row, dt_row)
        o_dir, new_state = _delta_call(layer, feat, gb, state_delta, new_state)
        x_all = _post_call(layer, layer == DEPTH - 1, x_all, o_dir, feat, mod, o_norm3, ws_bf, bs_t,
                           w_out_bf, norm_ffn3, w_ff1_bf, w_ff2_bf, norm_final2)

    y_prompt = x_all[:N_CTX].reshape(CTX_B, CTX_T, D_MODEL)
    y_sample = x_all[N_CTX:].reshape(LAT_B, LAT_T, D_MODEL)
    return (y_prompt, y_sample, new_state)
```

```python
import functools

import jax
import jax.numpy as jnp
from jax import lax
from jax.experimental import pallas as pl
from jax.experimental.pallas import tpu as pltpu

F32 = jnp.float32
BF16 = jnp.bfloat16

D_MODEL = 1024
DEPTH = 4
N_MOD = 6
H_A = 4
DK = 128
DV = 128
D_A = H_A * DK
G_B = 4
C_B = 128
D_B = G_B * C_B
CHUNK_MLP = 128
D_FF = 4 * D_MODEL
EPS = 1e-6

CTX_B, CTX_T = 32, 256
LAT_B, LAT_T = 2, 4096
N_CTX = CTX_B * CTX_T
N_LAT = LAT_B * LAT_T
N_TOK = N_CTX + N_LAT

TM = 512
N_TILES = N_TOK // TM
CTX_TILES = N_CTX // TM
LAT_TILES_PER_SEQ = LAT_T // TM
HALO = 8

CHUNK = 128
INV_BASE = 8
TD = 512
CHUNKS_PER_TD = TD // CHUNK
SEQ_PER_CTX_DTILE = TD // CTX_T
CTX_DTILES = N_CTX // TD
LAT_DTILES_PER_SEQ = LAT_T // TD
N_DTILES = N_TOK // TD

N_FEAT = 6 * D_A
VMEM_LIMIT_BYTES = 60000 * 1024


def _sigmoid(x):
    return 1.0 / (1.0 + jnp.exp(-x))


def _gelu_tanh(x):
    c = 0.7978845608028654
    return x * (0.5 * (1.0 + jnp.tanh(c * (x + 0.044715 * (x * x * x)))))


def _softplus(x):
    return jnp.maximum(x, 0.0) + jnp.log1p(jnp.exp(-jnp.abs(x)))


def _dot(a, b):
    return jnp.dot(a, b, preferred_element_type=F32)


def _dot_nt(a, b):
    return lax.dot_general(a, b, (((1,), (1,)), ((), ())), preferred_element_type=F32)


def _dot_tn(a, b):
    return lax.dot_general(a, b, (((0,), (0,)), ((), ())), preferred_element_type=F32)


def _mod_kernel(cond_ref, w_ref, b_ref, o_ref):
    c = cond_ref[...]
    s = (c * _sigmoid(c)).astype(BF16)
    o_ref[0] = _dot(s, w_ref[0].astype(BF16)) + b_ref[0]


def _modulation(cond8, w_mod, b_mod):
    tn = 1536
    return pl.pallas_call(
        _mod_kernel,
        grid=(DEPTH, (N_MOD * D_MODEL) // tn),
        in_specs=[
            pl.BlockSpec((8, D_MODEL), lambda l, j: (0, 0)),
            pl.BlockSpec((1, D_MODEL, tn), lambda l, j: (l, 0, j)),
            pl.BlockSpec((1, 1, tn), lambda l, j: (l, 0, j)),
        ],
        out_specs=pl.BlockSpec((1, 8, tn), lambda l, j: (l, 0, j)),
        out_shape=jax.ShapeDtypeStruct((DEPTH, 8, N_MOD * D_MODEL), F32),
        compiler_params=pltpu.CompilerParams(
            dimension_semantics=("arbitrary", "arbitrary"), vmem_limit_bytes=VMEM_LIMIT_BYTES),
        name="modulation",
    )(cond8, w_mod, b_mod.reshape(DEPTH, 1, N_MOD * D_MODEL))


def _mod_row(i):
    return jnp.where(i < CTX_TILES, 0, 1 + (i - CTX_TILES) // LAT_TILES_PER_SEQ)


def _pre_kernel(x_ref, xp_ref, xn_ref, mod_ref, nw_ref, w_ref, wab_ref, conv_ref, arow_ref,
                dtrow_ref, feat_ref, gb_ref):
    i = pl.program_id(0)
    m = mod_ref[0]
    sh1 = m[:, 0:D_MODEL]
    sc1 = m[:, D_MODEL:2 * D_MODEL]
    nw = nw_ref[...]

    def norm_mod(x):
        ms = jnp.mean(x * x, axis=-1, keepdims=True)
        y = x * lax.rsqrt(ms + EPS) * nw
        return (y * (1.0 + sc1) + sh1).astype(BF16)

    h = norm_mod(x_ref[...])
    hh = norm_mod(jnp.concatenate([xp_ref[...], xn_ref[...]], axis=0))
    h_ext = jnp.concatenate([h, hh], axis=0)

    r = lax.broadcasted_iota(jnp.int32, (TM, 1), 0)
    len_mask = jnp.where(i < CTX_TILES, CTX_T - 1, LAT_T - 1)
    pos = (i * TM + r) & len_mask
    is_start = pos == 0
    is_end = pos == len_mask

    cw = conv_ref[...]
    for blk in range(3):
        cols = slice(blk * D_A, (blk + 1) * D_A)
        pe = _dot(h_ext, w_ref[:, cols])
        p = pe[0:TM]
        prev_row = pe[TM + HALO - 1:TM + HALO]
        next_row = pe[TM + HALO:TM + HALO + 1]
        xm1 = jnp.where(r == 0, prev_row, pltpu.roll(p, 1, 0))
        xm1 = jnp.where(is_start, 0.0, xm1)
        xp1 = jnp.where(r == TM - 1, next_row, pltpu.roll(p, TM - 1, 0))
        xp1 = jnp.where(is_end, 0.0, xp1)
        y = cw[0:1, cols] * xm1 + cw[1:2, cols] * p + cw[2:3, cols] * xp1
        y = y * _sigmoid(y)
        if blk < 2:
            for hd in range(H_A):
                yh = y[:, hd * DK:(hd + 1) * DK]
                yh = yh * lax.rsqrt(jnp.sum(yh * yh, axis=-1, keepdims=True) + EPS)
                if blk == 0:
                    yh = yh * (DK ** -0.5)
                feat_ref[:, blk * D_A + hd * DK:blk * D_A + (hd + 1) * DK] = yh.astype(BF16)
        else:
            feat_ref[:, cols] = y.astype(BF16)

    gate = _dot(h, w_ref[:, 3 * D_A:4 * D_A])
    feat_ref[:, 3 * D_A:4 * D_A] = (gate * _sigmoid(gate)).astype(BF16)
    u = _dot(h, w_ref[:, 4 * D_A:5 * D_A])
    feat_ref[:, 4 * D_A:5 * D_A] = _gelu_tanh(u).astype(BF16)
    vg = _gelu_tanh(_dot(h, w_ref[:, 5 * D_A:6 * D_A]))
    for g in range(G_B):
        vgg = vg[:, g * C_B:(g + 1) * C_B]
        mu = jnp.mean(vgg, axis=-1, keepdims=True)
        dv = vgg - mu
        var = jnp.mean(dv * dv, axis=-1, keepdims=True)
        feat_ref[:, 5 * D_A + g * C_B:5 * D_A + (g + 1) * C_B] = (dv * lax.rsqrt(var + EPS)).astype(BF16)

    ab = _dot(h, wab_ref[...])
    lane = lax.broadcasted_iota(jnp.int32, (TM, 128), 1)
    beta = _sigmoid(ab)
    g_log = -jnp.exp(arow_ref[...]) * _softplus(ab + dtrow_ref[...])
    gb_ref[...] = jnp.where(lane < 2 * H_A, beta, g_log)


def _pre_call(layer, x_all, mod, norm_mix, w_main, w_ab, conv_qkv, a_row, dt_row):
    blocks_per_tile = TM // HALO
    n_halo_blocks = N_TOK // HALO
    const = dict(pipeline_mode=pl.Buffered(1))
    return pl.pallas_call(
        _pre_kernel,
        grid=(N_TILES,),
        in_specs=[
            pl.BlockSpec((TM, D_MODEL), lambda i: (i, 0)),
            pl.BlockSpec((HALO, D_MODEL), lambda i: (jnp.maximum(i * blocks_per_tile - 1, 0), 0)),
            pl.BlockSpec((HALO, D_MODEL),
                         lambda i: (jnp.minimum((i + 1) * blocks_per_tile, n_halo_blocks - 1), 0)),
            pl.BlockSpec((None, 1, 1, N_MOD * D_MODEL), lambda i: (layer, _mod_row(i), 0, 0)),
            pl.BlockSpec((None, 1, D_MODEL), lambda i: (layer, 0, 0), **const),
            pl.BlockSpec((None, D_MODEL, N_FEAT), lambda i: (layer, 0, 0), **const),
            pl.BlockSpec((None, D_MODEL, 128), lambda i: (layer, 0, 0), **const),
            pl.BlockSpec((None, 3, 3 * D_A), lambda i: (layer, 0, 0), **const),
            pl.BlockSpec((None, 1, 128), lambda i: (layer, 0, 0), **const),
            pl.BlockSpec((None, 1, 128), lambda i: (layer, 0, 0), **const),
        ],
        out_specs=[
            pl.BlockSpec((TM, N_FEAT), lambda i: (i, 0)),
            pl.BlockSpec((TM, 128), lambda i: (i, 0)),
        ],
        out_shape=[
            jax.ShapeDtypeStruct((N_TOK, N_FEAT), BF16),
            jax.ShapeDtypeStruct((N_TOK, 128), F32),
        ],
        compiler_params=pltpu.CompilerParams(
            dimension_semantics=("arbitrary",), vmem_limit_bytes=VMEM_LIMIT_BYTES),
        name=f"pre_l{layer}",
    )(x_all, x_all, x_all, mod, norm_mix, w_main, w_ab, conv_qkv, a_row, dt_row)


def _delta_tile_row(d, t):
    lat = t - CTX_DTILES
    b = lat // LAT_DTILES_PER_SEQ
    j = lat % LAT_DTILES_PER_SEQ
    j = jnp.where(d == 1, LAT_DTILES_PER_SEQ - 1 - j, j)
    return jnp.where(t < CTX_DTILES, t, CTX_DTILES + b * LAT_DTILES_PER_SEQ + j)


def _mm_inv(a, b):
    return _dot(a.astype(BF16), b.astype(BF16))


def _delta_kernel(q_ref, k_ref, v_ref, gb_ref, s0_ref, ns_in_ref, o_ref, ns_ref, s_scr):
    del ns_in_ref
    d = pl.program_id(0)
    t = pl.program_id(1)
    rev = d == 1
    is_ctx = t < CTX_DTILES

    @pl.when(is_ctx)
    def _():
        s_scr[...] = jnp.zeros_like(s_scr)

    @pl.when(jnp.logical_and(t >= CTX_DTILES, (t - CTX_DTILES) % LAT_DTILES_PER_SEQ == 0))
    def _():
        s_scr[...] = s0_ref[...]

    row = lax.broadcasted_iota(jnp.int32, (CHUNK, CHUNK), 0)
    col = lax.broadcasted_iota(jnp.int32, (CHUNK, CHUNK), 1)
    sgn = jnp.where(rev, -1, 1)
    incl = (row - col) * sgn >= 0
    strict = (row - col) * sgn > 0
    eye = (row == col).astype(F32)
    base_mask = (row // INV_BASE) == (col // INV_BASE)
    off_masks = []
    blk = INV_BASE
    while blk < CHUNK:
        off_masks.append(jnp.logical_and((row // (2 * blk)) == (col // (2 * blk)),
                                         (row // blk) != (col // blk)))
        blk *= 2
    rix = lax.broadcasted_iota(jnp.int32, (CHUNK, 128), 0)

    probs = []
    for c in range(CHUNKS_PER_TD):
        cc = jnp.where(rev, CHUNKS_PER_TD - 1 - c, c)
        rows = pl.ds(pl.multiple_of(cc * CHUNK, CHUNK), CHUNK)
        gb = gb_ref[rows, :]
        fwd_sum = gb
        s = 1
        while s < CHUNK:
            fwd_sum = fwd_sum + jnp.where(rix >= s, pltpu.roll(fwd_sum, s, 0), 0.0)
            s *= 2
        total = fwd_sum[CHUNK - 1:CHUNK, :]
        cum = jnp.where(rev, total - fwd_sum + gb, fwd_sum)
        for hd in range(H_A):
            beta = jnp.where(rev, gb[:, H_A + hd:H_A + hd + 1], gb[:, hd:hd + 1])
            g_c = jnp.where(rev, cum[:, 3 * H_A + hd:3 * H_A + hd + 1],
                            cum[:, 2 * H_A + hd:2 * H_A + hd + 1])
            g_tot = jnp.where(rev, total[:, 3 * H_A + hd:3 * H_A + hd + 1],
                              total[:, 2 * H_A + hd:2 * H_A + hd + 1])
            g_b = jnp.broadcast_to(g_c, (CHUNK, CHUNK))
            diff = g_b - g_b.T
            decay = jnp.where(incl, jnp.exp(jnp.where(incl, diff, 0.0)), 0.0)
            eg = jnp.exp(g_c)
            hcols = slice(hd * DK, (hd + 1) * DK)
            q_bf = q_ref[rows, hcols]
            k_bf = k_ref[rows, hcols]
            k = k_bf.astype(F32)
            kb = k * beta
            probs.append(dict(
                c=c, hd=hd, rows=rows, hcols=hcols, decay=decay, k_bf=k_bf,
                kq_lhs=jnp.concatenate([kb.astype(BF16), q_bf], axis=0),
                rhs=jnp.concatenate([v_ref[rows, hcols].astype(F32) * beta, kb * eg], axis=-1),
                q_dec=(q_bf.astype(F32) * eg).astype(BF16),
                k_dec=(k * jnp.exp(g_tot - g_c)).astype(BF16),
                g_tot=g_tot))

    for p in probs:
        kq = _dot_nt(p["kq_lhs"], p["k_bf"])
        p["a"] = jnp.where(strict, kq[0:CHUNK] * p["decay"], 0.0)
        p["qk"] = jnp.where(incl, kq[CHUNK:2 * CHUNK] * p["decay"], 0.0).astype(BF16)

    for p in probs:
        a0 = jnp.where(base_mask, p["a"], 0.0)
        p["inv"] = eye - a0
        p["pw"] = a0
    n = 2
    while n < INV_BASE:
        for p in probs:
            p["pw"] = _mm_inv(p["pw"], p["pw"])
        for p in probs:
            p["inv"] = p["inv"] + _mm_inv(p["inv"], p["pw"])
        n *= 2
    for off_mask in off_masks:
        for p in probs:
            p["tmp"] = _mm_inv(p["inv"], jnp.where(off_mask, p["a"], 0.0))
        for p in probs:
            p["inv"] = p["inv"] - _mm_inv(p["tmp"], p["inv"])
    for p in probs:
        sol = _mm_inv(p["inv"], p["rhs"])
        p["u"] = sol[:, 0:DV]
        p["wq_lhs"] = jnp.concatenate([sol[:, DV:2 * DV].astype(BF16), p["q_dec"]], axis=0)

    state = [s_scr[hd] for hd in range(H_A)]
    mid_state = None
    for c in range(CHUNKS_PER_TD):
        if c == CHUNKS_PER_TD // 2:
            mid_state = state
            state = [jnp.where(is_ctx, 0.0, st) for st in state]
        cps = [p for p in probs if p["c"] == c]
        for p in cps:
            p["ws_qs"] = _dot(p["wq_lhs"], state[p["hd"]].astype(BF16))
        for p in cps:
            v_new = (p["u"] - p["ws_qs"][0:CHUNK]).astype(BF16)
            o_ref[p["rows"], p["hcols"]] = p["ws_qs"][CHUNK:2 * CHUNK] + _dot(p["qk"], v_new)
            state[p["hd"]] = state[p["hd"]] * jnp.exp(p["g_tot"]) + _dot_tn(p["k_dec"], v_new)
    for hd in range(H_A):
        s_scr[hd] = state[hd]

    @pl.when(is_ctx)
    def _():
        for hd in range(H_A):
            ns_ref[0, hd] = jnp.where(rev, state[hd], mid_state[hd])
            ns_ref[1, hd] = jnp.where(rev, mid_state[hd], state[hd])


def _delta_call(layer, feat, gb, state_delta, new_state):
    def feat_spec(col_block):
        return pl.BlockSpec((TD, D_A), lambda d, t: (_delta_tile_row(d, t), col_block))

    def s0_index(d, t):
        b = jnp.maximum(t - CTX_DTILES, 0) // LAT_DTILES_PER_SEQ
        return (b, layer, d, 0, 0, 0)

    def ns_index(d, t):
        return (jnp.minimum(t, CTX_DTILES - 1), layer, d, 0, 0, 0)

    return pl.pallas_call(
        _delta_kernel,
        grid=(2, N_DTILES),
        in_specs=[
            feat_spec(0), feat_spec(1), feat_spec(2),
            pl.BlockSpec((TD, 128), lambda d, t: (_delta_tile_row(d, t), 0)),
            pl.BlockSpec((None, None, None, H_A, DK, DV), s0_index),
            pl.BlockSpec(memory_space=pl.ANY),
        ],
        out_specs=[
            pl.BlockSpec((None, TD, D_A), lambda d, t: (d, _delta_tile_row(d, t), 0)),
            pl.BlockSpec((SEQ_PER_CTX_DTILE, None, None, H_A, DK, DV), ns_index),
        ],
        out_shape=[
            jax.ShapeDtypeStruct((2, N_TOK, D_A), F32),
            jax.ShapeDtypeStruct(new_state.shape, F32),
        ],
        scratch_shapes=[pltpu.VMEM((H_A, DK, DV), F32)],
        input_output_aliases={5: 1},
        compiler_params=pltpu.CompilerParams(
            dimension_semantics=("arbitrary", "arbitrary"), vmem_limit_bytes=VMEM_LIMIT_BYTES),
        name=f"delta_l{layer}",
    )(feat, feat, feat, gb, state_delta, new_state)


def _post_kernel(final, x_ref, of_ref, ob_ref, feat_ref, mod_ref, onorm_ref, ws_ref, bs_ref,
                 wout_ref, nffn_ref, wff1_ref, wff2_ref, nfin_ref, y_ref):
    m = mod_ref[0]
    ga1 = m[:, 2 * D_MODEL:3 * D_MODEL]
    sh2 = m[:, 3 * D_MODEL:4 * D_MODEL]
    sc2 = m[:, 4 * D_MODEL:5 * D_MODEL]
    ga2 = m[:, 5 * D_MODEL:6 * D_MODEL]

    o = of_ref[...] + ob_ref[...]
    onw = onorm_ref[...]
    mix_in = []
    for hd in range(H_A):
        oh = o[:, hd * DV:(hd + 1) * DV]
        oh = oh * lax.rsqrt(jnp.mean(oh * oh, axis=-1, keepdims=True) + EPS) * onw
        gate = feat_ref[:, hd * DV:(hd + 1) * DV].astype(F32)
        mix_in.append((oh * gate).astype(BF16))
    bs = bs_ref[...]
    for g in range(G_B):
        parts = []
        for c in range(TM // CHUNK_MLP):
            rows = slice(c * CHUNK_MLP, (c + 1) * CHUNK_MLP)
            vgn = feat_ref[rows, 2 * D_A + g * C_B:2 * D_A + (g + 1) * C_B]
            mixed = _dot(ws_ref[g], vgn) + bs[:, g:g + 1]
            ug = feat_ref[rows, D_A + g * C_B:D_A + (g + 1) * C_B].astype(F32)
            parts.append((ug * mixed).astype(BF16))
        mix_in.append(jnp.concatenate(parts, axis=0))
    mix = jnp.concatenate(mix_in, axis=-1)
    x1 = x_ref[...] + ga1 * _dot(mix, wout_ref[...])

    ms = jnp.mean(x1 * x1, axis=-1, keepdims=True)
    h2 = (x1 * lax.rsqrt(ms + EPS) * nffn_ref[...] * (1.0 + sc2) + sh2).astype(BF16)
    a = jnp.maximum(_dot(h2, wff1_ref[...]), 0.0)
    x2 = x1 + ga2 * _dot((a * a).astype(BF16), wff2_ref[...])
    if final:
        ms2 = jnp.mean(x2 * x2, axis=-1, keepdims=True)
        x2 = x2 * lax.rsqrt(ms2 + EPS) * nfin_ref[...]
    y_ref[...] = x2


def _post_call(layer, final, x_all, o_dir, feat, mod, o_norm, ws, bs_t, w_out, norm_ffn, w_ff1,
               w_ff2, norm_final):
    const = dict(pipeline_mode=pl.Buffered(1))
    return pl.pallas_call(
        functools.partial(_post_kernel, final),
        grid=(N_TILES,),
        in_specs=[
            pl.BlockSpec((TM, D_MODEL), lambda i: (i, 0)),
            pl.BlockSpec((None, TM, D_A), lambda i: (0, i, 0)),
            pl.BlockSpec((None, TM, D_A), lambda i: (1, i, 0)),
            pl.BlockSpec((TM, 3 * D_A), lambda i: (i, 1)),
            pl.BlockSpec((None, 1, 1, N_MOD * D_MODEL), lambda i: (layer, _mod_row(i), 0, 0)),
            pl.BlockSpec((None, 1, DV), lambda i: (layer, 0, 0), **const),
            pl.BlockSpec((None, G_B, CHUNK_MLP, CHUNK_MLP), lambda i: (layer, 0, 0, 0), **const),
            pl.BlockSpec((None, CHUNK_MLP, 128), lambda i: (layer, 0, 0), **const),
            pl.BlockSpec((None, D_MODEL, D_MODEL), lambda i: (layer, 0, 0), **const),
            pl.BlockSpec((None, 1, D_MODEL), lambda i: (layer, 0, 0), **const),
            pl.BlockSpec((None, D_MODEL, D_FF), lambda i: (layer, 0, 0), **const),
            pl.BlockSpec((None, D_FF, D_MODEL), lambda i: (layer, 0, 0), **const),
            pl.BlockSpec((1, D_MODEL), lambda i: (0, 0), **const),
        ],
        out_specs=pl.BlockSpec((TM, D_MODEL), lambda i: (i, 0)),
        out_shape=jax.ShapeDtypeStruct((N_TOK, D_MODEL), F32),
        compiler_params=pltpu.CompilerParams(
            dimension_semantics=("arbitrary",), vmem_limit_bytes=VMEM_LIMIT_BYTES),
        name=f"post_l{layer}",
    )(x_all, o_dir, o_dir, feat, mod, o_norm, ws, bs_t, w_out, norm_ffn, w_ff1, w_ff2, norm_final)


def kernel(x_prompt, x_sample, state_delta, c, c_ctx, w_mod, b_mod, norm_mix, w_in, conv_qkv, a_log,
           dt_bias, o_norm, w_spatial, b_spatial, w_out, norm_ffn, w_ff1, w_ff2, norm_final):
    x_all = jnp.concatenate(
        [x_prompt.reshape(N_CTX, D_MODEL), x_sample.reshape(N_LAT, D_MODEL)], axis=0)
    cond8 = jnp.concatenate([c_ctx[None, :], c, jnp.zeros((8 - 1 - LAT_B, D_MODEL), F32)], axis=0)

    n_qkvg = 4 * D_A
    w_main = jnp.concatenate([w_in[:, :, :n_qkvg], w_in[:, :, n_qkvg + 4 * H_A:]], axis=-1).astype(BF16)
    w_ab = jnp.pad(w_in[:, :, n_qkvg:n_qkvg + 4 * H_A], ((0, 0), (0, 0), (0, 128 - 4 * H_A))).astype(BF16)
    a_row = jnp.pad(a_log.reshape(DEPTH, 1, 2 * H_A), ((0, 0), (0, 0), (2 * H_A, 128 - 4 * H_A)))
    dt_row = jnp.pad(dt_bias.reshape(DEPTH, 1, 2 * H_A), ((0, 0), (0, 0), (2 * H_A, 128 - 4 * H_A)))
    bs_t = jnp.pad(jnp.swapaxes(b_spatial, 1, 2), ((0, 0), (0, 0), (0, 128 - G_B)))
    ws_bf = w_spatial.astype(BF16)
    w_out_bf = w_out.astype(BF16)
    w_ff1_bf = w_ff1.astype(BF16)
    w_ff2_bf = w_ff2.astype(BF16)
    norm_mix3 = norm_mix.reshape(DEPTH, 1, D_MODEL)
    norm_ffn3 = norm_ffn.reshape(DEPTH, 1, D_MODEL)
    o_norm3 = o_norm.reshape(DEPTH, 1, DV)
    norm_final2 = norm_final.reshape(1, D_MODEL)

    mod = _modulation(cond8, w_mod, b_mod).reshape(DEPTH, 8, 1, N_MOD * D_MODEL)

    new_state = jnp.zeros((CTX_B, DEPTH, 2, H_A, DK, DV), F32)
    for layer in range(DEPTH):
        feat, gb = _pre_call(layer, x_all, mod, norm_mix3, w_main, w_ab, conv_qkv, a_row, dt_row)
        o_dir, new_state = _delta_call(layer, feat, gb, state_delta, new_state)
        x_all = _post_call(layer, layer == DEPTH - 1, x_all, o_dir, feat, mod, o_norm3, ws_bf, bs_t,
                           w_out_bf, norm_ffn3, w_ff1_bf, w_ff2_bf, norm_final2)

    y_prompt = x_all[:N_CTX].reshape(CTX_B, CTX_T, D_MODEL)
    y_sample = x_all[N_CTX:].reshape(LAT_B, LAT_T, D_MODEL)
    return (y_prompt, y_sample, new_state)
```

```python
import functools

import jax
import jax.numpy as jnp
from jax import lax
from jax.experimental import pallas as pl
from jax.experimental.pallas import tpu as pltpu

F32 = jnp.float32
BF16 = jnp.bfloat16

D_MODEL = 1024
DEPTH = 4
N_MOD = 6
H_A = 4
DK = 128
DV = 128
D_A = H_A * DK
G_B = 4
C_B = 128
D_B = G_B * C_B
CHUNK_MLP = 128
D_FF = 4 * D_MODEL
EPS = 1e-6

CTX_B, CTX_T = 32, 256
LAT_B, LAT_T = 2, 4096
N_CTX = CTX_B * CTX_T
N_LAT = LAT_B * LAT_T
N_TOK = N_CTX + N_LAT

TM = 512
N_TILES = N_TOK // TM
CTX_TILES = N_CTX // TM
LAT_TILES_PER_SEQ = LAT_T // TM
HALO = 8
POST_SUB = 256

CHUNK = 128
INV_BASE = 8
TD = 512
CHUNKS_PER_TD = TD // CHUNK
SEQ_PER_CTX_DTILE = TD // CTX_T
CTX_DTILES = N_CTX // TD
LAT_DTILES_PER_SEQ = LAT_T // TD
N_DTILES = N_TOK // TD

N_FEAT = 6 * D_A
VMEM_LIMIT_BYTES = 60000 * 1024


def _sigmoid(x):
    return 1.0 / (1.0 + jnp.exp(-x))


def _silu(x):
    hx = 0.5 * x
    return hx + hx * jnp.tanh(hx)


_GELU_C0 = 0.7978845608028654
_GELU_C1 = _GELU_C0 * 0.044715


def _gelu_tanh(x):
    hx = 0.5 * x
    return hx + hx * jnp.tanh(x * (_GELU_C0 + _GELU_C1 * (x * x)))


def _softplus(x):
    return jnp.maximum(x, 0.0) + jnp.log1p(jnp.exp(-jnp.abs(x)))


def _dot(a, b):
    return jnp.dot(a, b, preferred_element_type=F32)


def _dot_nt(a, b):
    return lax.dot_general(a, b, (((1,), (1,)), ((), ())), preferred_element_type=F32)


def _dot_tn(a, b):
    return lax.dot_general(a, b, (((0,), (0,)), ((), ())), preferred_element_type=F32)


def _mod_kernel(cond_ref, w_ref, b_ref, o_ref):
    c = cond_ref[...]
    s = (c * _sigmoid(c)).astype(BF16)
    o_ref[0] = _dot(s, w_ref[0].astype(BF16)) + b_ref[0]


def _modulation(cond8, w_mod, b_mod):
    tn = 1536
    return pl.pallas_call(
        _mod_kernel,
        grid=(DEPTH, (N_MOD * D_MODEL) // tn),
        in_specs=[
            pl.BlockSpec((8, D_MODEL), lambda l, j: (0, 0)),
            pl.BlockSpec((1, D_MODEL, tn), lambda l, j: (l, 0, j)),
            pl.BlockSpec((1, 1, tn), lambda l, j: (l, 0, j)),
        ],
        out_specs=pl.BlockSpec((1, 8, tn), lambda l, j: (l, 0, j)),
        out_shape=jax.ShapeDtypeStruct((DEPTH, 8, N_MOD * D_MODEL), F32),
        compiler_params=pltpu.CompilerParams(
            dimension_semantics=("arbitrary", "arbitrary"), vmem_limit_bytes=VMEM_LIMIT_BYTES),
        name="modulation",
    )(cond8, w_mod, b_mod.reshape(DEPTH, 1, N_MOD * D_MODEL))


def _mod_row(i):
    return jnp.where(i < CTX_TILES, 0, 1 + (i - CTX_TILES) // LAT_TILES_PER_SEQ)


def _pre_kernel(split, *refs):
    i = pl.program_id(0)
    is_ctx = i < CTX_TILES
    if split:
        (xc_ref, xl_ref, xcp_ref, xlp_ref, xcn_ref, xln_ref, mod_ref, nw_ref, w_ref, wab_ref,
         conv_ref, arow_ref, dtrow_ref, qkv_ref, rest_ref, gb_ref) = refs
        x = jnp.where(is_ctx, xc_ref[...], xl_ref[...])
        x_prev = jnp.where(is_ctx, xcp_ref[...], xlp_ref[...])
        x_next = jnp.where(is_ctx, xcn_ref[...], xln_ref[...])
    else:
        (x_ref, xp_ref, xn_ref, mod_ref, nw_ref, w_ref, wab_ref,
         conv_ref, arow_ref, dtrow_ref, qkv_ref, rest_ref, gb_ref) = refs
        x, x_prev, x_next = x_ref[...], xp_ref[...], xn_ref[...]
    m = mod_ref[0]
    sh1 = m[:, 0:D_MODEL]
    coef = nw_ref[...] * (1.0 + m[:, D_MODEL:2 * D_MODEL])

    def norm_mod(xx):
        ms = jnp.mean(xx * xx, axis=-1, keepdims=True)
        return ((xx * lax.rsqrt(ms + EPS)) * coef + sh1).astype(BF16)

    h = norm_mod(x)
    hh = norm_mod(jnp.concatenate([x_prev, x_next], axis=0))
    h_ext = jnp.concatenate([h, hh], axis=0)

    lat_j = (i - CTX_TILES) % LAT_TILES_PER_SEQ
    starts_seq = jnp.logical_or(is_ctx, lat_j == 0)
    ends_seq = jnp.logical_or(is_ctx, lat_j == LAT_TILES_PER_SEQ - 1)
    r8 = lax.broadcasted_iota(jnp.int32, (HALO, 1), 0)

    def shifted(p, prev_row, next_row):
        dn = pltpu.roll(p, 1, 0)
        up = pltpu.roll(p, TM - 1, 0)
        dn_parts, up_parts = [], []
        for b in range(0, TM, CTX_T):
            slab = dn[b:b + HALO]
            if b == 0:
                slab = jnp.where(r8 == 0, jnp.where(starts_seq, 0.0, prev_row), slab)
            else:
                slab = jnp.where(jnp.logical_and(r8 == 0, is_ctx), 0.0, slab)
            dn_parts += [slab, dn[b + HALO:b + CTX_T]]
            e = b + CTX_T
            slab = up[e - HALO:e]
            if e == TM:
                slab = jnp.where(r8 == HALO - 1, jnp.where(ends_seq, 0.0, next_row), slab)
            else:
                slab = jnp.where(jnp.logical_and(r8 == HALO - 1, is_ctx), 0.0, slab)
            up_parts += [up[b:e - HALO], slab]
        return jnp.concatenate(dn_parts, axis=0), jnp.concatenate(up_parts, axis=0)

    cw = conv_ref[...]
    for blk in range(3):
        cols = slice(blk * D_A, (blk + 1) * D_A)
        pe = _dot(h_ext, w_ref[:, cols])
        p = pe[0:TM]
        xm1, xp1 = shifted(p, pe[TM + HALO - 1:TM + HALO], pe[TM + HALO:TM + HALO + 1])
        y = _silu(cw[0:1, cols] * xm1 + cw[1:2, cols] * p + cw[2:3, cols] * xp1)
        if blk < 2:
            scale = DK ** -0.5 if blk == 0 else 1.0
            for hd in range(H_A):
                yh = y[:, hd * DK:(hd + 1) * DK]
                inv_norm = lax.rsqrt(jnp.sum(yh * yh, axis=-1, keepdims=True) + EPS) * scale
                qkv_ref[:, blk * D_A + hd * DK:blk * D_A + (hd + 1) * DK] = (yh * inv_norm).astype(BF16)
        else:
            qkv_ref[:, cols] = y.astype(BF16)

    rest_ref[...] = _dot(h, w_ref[:, 3 * D_A:6 * D_A]).astype(BF16)

    ab = _dot(h, wab_ref[...])
    lane = lax.broadcasted_iota(jnp.int32, (TM, 128), 1)
    beta = _sigmoid(ab)
    g_log = -jnp.exp(arow_ref[...]) * _softplus(ab + dtrow_ref[...])
    gb_ref[...] = jnp.where(lane < 2 * H_A, beta, g_log)


def _pre_call(layer, xs, mod, norm_mix, w_main, w_ab, conv_qkv, a_row, dt_row):
    split = len(xs) == 2
    bpt = TM // HALO
    const = dict(pipeline_mode=pl.Buffered(1))

    def tile_specs(first_tile, n_tiles):
        def local(i):
            return jnp.clip(i - first_tile, 0, n_tiles - 1)
        return (
            pl.BlockSpec((TM, D_MODEL), lambda i: (local(i), 0)),
            pl.BlockSpec((HALO, D_MODEL), lambda i: (jnp.maximum(local(i) * bpt - 1, 0), 0)),
            pl.BlockSpec((HALO, D_MODEL),
                         lambda i: (jnp.minimum((local(i) + 1) * bpt, n_tiles * bpt - 1), 0)),
        )

    if split:
        c_specs = tile_specs(0, CTX_TILES)
        l_specs = tile_specs(CTX_TILES, N_TILES - CTX_TILES)
        x_specs = [c_specs[0], l_specs[0], c_specs[1], l_specs[1], c_specs[2], l_specs[2]]
        x_args = [xs[0], xs[1]] * 3
    else:
        x_specs = list(tile_specs(0, N_TILES))
        x_args = [xs[0]] * 3
    return pl.pallas_call(
        functools.partial(_pre_kernel, split),
        grid=(N_TILES,),
        in_specs=x_specs + [
            pl.BlockSpec((None, 1, 1, N_MOD * D_MODEL), lambda i: (layer, _mod_row(i), 0, 0)),
            pl.BlockSpec((None, 1, D_MODEL), lambda i: (layer, 0, 0), **const),
            pl.BlockSpec((None, D_MODEL, N_FEAT), lambda i: (layer, 0, 0), **const),
            pl.BlockSpec((None, D_MODEL, 128), lambda i: (layer, 0, 0), **const),
            pl.BlockSpec((None, 3, 3 * D_A), lambda i: (layer, 0, 0), **const),
            pl.BlockSpec((None, 1, 128), lambda i: (layer, 0, 0), **const),
            pl.BlockSpec((None, 1, 128), lambda i: (layer, 0, 0), **const),
        ],
        out_specs=[
            pl.BlockSpec((TM, 3 * D_A), lambda i: (i, 0)),
            pl.BlockSpec((TM, 3 * D_A), lambda i: (i, 0)),
            pl.BlockSpec((TM, 128), lambda i: (i, 0)),
        ],
        out_shape=[
            jax.ShapeDtypeStruct((N_TOK, 3 * D_A), BF16),
            jax.ShapeDtypeStruct((N_TOK, 3 * D_A), BF16),
            jax.ShapeDtypeStruct((N_TOK, 128), F32),
        ],
        compiler_params=pltpu.CompilerParams(
            dimension_semantics=("arbitrary",), vmem_limit_bytes=VMEM_LIMIT_BYTES),
        name=f"pre_l{layer}",
    )(*x_args, mod, norm_mix, w_main, w_ab, conv_qkv, a_row, dt_row)


def _delta_tile_row(d, t):
    lat = t - CTX_DTILES
    b = lat // LAT_DTILES_PER_SEQ
    j = lat % LAT_DTILES_PER_SEQ
    j = jnp.where(d == 1, LAT_DTILES_PER_SEQ - 1 - j, j)
    return jnp.where(t < CTX_DTILES, t, CTX_DTILES + b * LAT_DTILES_PER_SEQ + j)


def _mm_inv(a, b):
    return _dot(a.astype(BF16), b.astype(BF16))


def _delta_kernel(q_ref, k_ref, v_ref, gb_ref, s0_ref, ns_in_ref, o_ref, ns_ref, s_scr):
    del ns_in_ref
    d = pl.program_id(0)
    t = pl.program_id(1)
    rev = d == 1
    is_ctx = t < CTX_DTILES

    @pl.when(is_ctx)
    def _():
        s_scr[...] = jnp.zeros_like(s_scr)

    @pl.when(jnp.logical_and(t >= CTX_DTILES, (t - CTX_DTILES) % LAT_DTILES_PER_SEQ == 0))
    def _():
        s_scr[...] = s0_ref[...]

    row = lax.broadcasted_iota(jnp.int32, (CHUNK, CHUNK), 0)
    col = lax.broadcasted_iota(jnp.int32, (CHUNK, CHUNK), 1)
    sgn = jnp.where(rev, -1, 1)
    incl = (row - col) * sgn >= 0
    strict = (row - col) * sgn > 0
    eye = (row == col).astype(F32)
    base_mask = (row // INV_BASE) == (col // INV_BASE)
    off_masks = []
    blk = INV_BASE
    while blk < CHUNK:
        off_masks.append(jnp.logical_and((row // (2 * blk)) == (col // (2 * blk)),
                                         (row // blk) != (col // blk)))
        blk *= 2
    rix = lax.broadcasted_iota(jnp.int32, (CHUNK, 128), 0)

    probs = []
    for c in range(CHUNKS_PER_TD):
        cc = jnp.where(rev, CHUNKS_PER_TD - 1 - c, c)
        rows = pl.ds(pl.multiple_of(cc * CHUNK, CHUNK), CHUNK)
        gb = gb_ref[rows, :]
        fwd_sum = gb
        s = 1
        while s < CHUNK:
            fwd_sum = fwd_sum + jnp.where(rix >= s, pltpu.roll(fwd_sum, s, 0), 0.0)
            s *= 2
        total = fwd_sum[CHUNK - 1:CHUNK, :]
        cum = jnp.where(rev, total - fwd_sum + gb, fwd_sum)
        for hd in range(H_A):
            beta = jnp.where(rev, gb[:, H_A + hd:H_A + hd + 1], gb[:, hd:hd + 1])
            g_c = jnp.where(rev, cum[:, 3 * H_A + hd:3 * H_A + hd + 1],
                            cum[:, 2 * H_A + hd:2 * H_A + hd + 1])
            g_tot = jnp.where(rev, total[:, 3 * H_A + hd:3 * H_A + hd + 1],
                              total[:, 2 * H_A + hd:2 * H_A + hd + 1])
            g_b = jnp.broadcast_to(g_c, (CHUNK, CHUNK))
            diff = g_b - g_b.T
            decay = jnp.where(incl, jnp.exp(jnp.where(incl, diff, 0.0)), 0.0)
            eg = jnp.exp(g_c)
            hcols = slice(hd * DK, (hd + 1) * DK)
            q_bf = q_ref[rows, hcols]
            k_bf = k_ref[rows, hcols]
            k = k_bf.astype(F32)
            kb = k * beta
            probs.append(dict(
                c=c, hd=hd, rows=rows, hcols=hcols, decay=decay, k_bf=k_bf,
                kq_lhs=jnp.concatenate([kb.astype(BF16), q_bf], axis=0),
                rhs=jnp.concatenate([v_ref[rows, hcols].astype(F32) * beta, kb * eg], axis=-1),
                q_dec=(q_bf.astype(F32) * eg).astype(BF16),
                k_dec=(k * jnp.exp(g_tot - g_c)).astype(BF16),
                g_tot=g_tot))

    for p in probs:
        kq = _dot_nt(p["kq_lhs"], p["k_bf"])
        p["a"] = jnp.where(strict, kq[0:CHUNK] * p["decay"], 0.0)
        p["qk"] = jnp.where(incl, kq[CHUNK:2 * CHUNK] * p["decay"], 0.0).astype(BF16)

    for p in probs:
        a0 = jnp.where(base_mask, p["a"], 0.0)
        p["inv"] = eye - a0
        p["pw"] = a0
    for p in probs:
        p["pw"] = _mm_inv(p["pw"], p["pw"])
    n = 2
    while 2 * n < INV_BASE:
        for p in probs:
            st = _mm_inv(jnp.concatenate([p["pw"], p["inv"]], axis=0), p["pw"])
            p["inv"] = p["inv"] + st[CHUNK:2 * CHUNK]
            p["pw"] = st[0:CHUNK]
        n *= 2
    for p in probs:
        p["inv"] = p["inv"] + _mm_inv(p["inv"], p["pw"])
    for off_mask in off_masks:
        for p in probs:
            p["tmp"] = _mm_inv(p["inv"], jnp.where(off_mask, p["a"], 0.0))
        for p in probs:
            p["inv"] = p["inv"] - _mm_inv(p["tmp"], p["inv"])
    for p in probs:
        sol = _mm_inv(p["inv"], p["rhs"])
        p["u"] = sol[:, 0:DV]
        p["wq_lhs"] = jnp.concatenate([sol[:, DV:2 * DV].astype(BF16), p["q_dec"]], axis=0)

    state = [s_scr[hd] for hd in range(H_A)]
    mid_state = None
    for c in range(CHUNKS_PER_TD):
        if c == CHUNKS_PER_TD // 2:
            mid_state = state
            state = [jnp.where(is_ctx, 0.0, st) for st in state]
        cps = [p for p in probs if p["c"] == c]
        for p in cps:
            p["ws_qs"] = _dot(p["wq_lhs"], state[p["hd"]].astype(BF16))
        for p in cps:
            v_new = (p["u"] - p["ws_qs"][0:CHUNK]).astype(BF16)
            o_ref[p["rows"], p["hcols"]] = p["ws_qs"][CHUNK:2 * CHUNK] + _dot(p["qk"], v_new)
            state[p["hd"]] = state[p["hd"]] * jnp.exp(p["g_tot"]) + _dot_tn(p["k_dec"], v_new)
    for hd in range(H_A):
        s_scr[hd] = state[hd]

    @pl.when(is_ctx)
    def _():
        for hd in range(H_A):
            ns_ref[0, hd] = jnp.where(rev, state[hd], mid_state[hd])
            ns_ref[1, hd] = jnp.where(rev, mid_state[hd], state[hd])


def _delta_call(layer, feat, gb, state_delta, new_state):
    def feat_spec(col_block):
        return pl.BlockSpec((TD, D_A), lambda d, t: (_delta_tile_row(d, t), col_block))

    def s0_index(d, t):
        b = jnp.maximum(t - CTX_DTILES, 0) // LAT_DTILES_PER_SEQ
        return (b, layer, d, 0, 0, 0)

    def ns_index(d, t):
        return (jnp.minimum(t, CTX_DTILES - 1), layer, d, 0, 0, 0)

    return pl.pallas_call(
        _delta_kernel,
        grid=(2, N_DTILES),
        in_specs=[
            feat_spec(0), feat_spec(1), feat_spec(2),
            pl.BlockSpec((TD, 128), lambda d, t: (_delta_tile_row(d, t), 0)),
            pl.BlockSpec((None, None, None, H_A, DK, DV), s0_index),
            pl.BlockSpec(memory_space=pl.ANY),
        ],
        out_specs=[
            pl.BlockSpec((None, TD, D_A), lambda d, t: (d, _delta_tile_row(d, t), 0)),
            pl.BlockSpec((SEQ_PER_CTX_DTILE, None, None, H_A, DK, DV), ns_index),
        ],
        out_shape=[
            jax.ShapeDtypeStruct((2, N_TOK, D_A), F32),
            jax.ShapeDtypeStruct(new_state.shape, F32),
        ],
        scratch_shapes=[pltpu.VMEM((H_A, DK, DV), F32)],
        input_output_aliases={5: 1},
        compiler_params=pltpu.CompilerParams(
            dimension_semantics=("arbitrary", "arbitrary"), vmem_limit_bytes=VMEM_LIMIT_BYTES),
        name=f"delta_l{layer}",
    )(feat, feat, feat, gb, state_delta, new_state)


def _post_kernel(final, split, *refs):
    is_ctx = pl.program_id(0) < CTX_TILES
    (of_ref, ob_ref, rest_ref, mod_ref, onorm_ref, ws_ref, bs_ref, wout_ref, nffn_ref, wff1_ref,
     wff2_ref, nfin_ref) = refs[1 + split:13 + split]
    out_refs = refs[13 + split:]
    m = mod_ref[0]
    ga1 = m[:, 2 * D_MODEL:3 * D_MODEL]
    sh2 = m[:, 3 * D_MODEL:4 * D_MODEL]
    coef2 = nffn_ref[...] * (1.0 + m[:, 4 * D_MODEL:5 * D_MODEL])
    ga2 = m[:, 5 * D_MODEL:6 * D_MODEL]
    onw = onorm_ref[...]
    bs = bs_ref[...]

    def head(rows):
        if split:
            x = jnp.where(is_ctx, refs[0][rows, :], refs[1][rows, :])
        else:
            x = refs[0][rows, :]
        o = of_ref[rows, :] + ob_ref[rows, :]
        mix_in = []
        for hd in range(H_A):
            oh = o[:, hd * DV:(hd + 1) * DV]
            oh = oh * (lax.rsqrt(jnp.mean(oh * oh, axis=-1, keepdims=True) + EPS) * onw)
            gate = _silu(rest_ref[rows, hd * DV:(hd + 1) * DV].astype(F32))
            mix_in.append((oh * gate).astype(BF16))
        for g in range(G_B):
            ug = _gelu_tanh(rest_ref[rows, D_A + g * C_B:D_A + (g + 1) * C_B].astype(F32))
            vg = _gelu_tanh(rest_ref[rows, 2 * D_A + g * C_B:2 * D_A + (g + 1) * C_B].astype(F32))
            dv = vg - jnp.mean(vg, axis=-1, keepdims=True)
            var = jnp.mean(dv * dv, axis=-1, keepdims=True)
            vgn = (dv * lax.rsqrt(var + EPS)).astype(BF16)
            parts = []
            for c in range(POST_SUB // CHUNK_MLP):
                crows = slice(c * CHUNK_MLP, (c + 1) * CHUNK_MLP)
                mixed = _dot(ws_ref[g], vgn[crows]) + bs[:, g:g + 1]
                parts.append((ug[crows] * mixed).astype(BF16))
            mix_in.append(jnp.concatenate(parts, axis=0))
        mix = jnp.concatenate(mix_in, axis=-1)
        x1 = x + ga1 * _dot(mix, wout_ref[...])
        ms = jnp.mean(x1 * x1, axis=-1, keepdims=True)
        return x1, ((x1 * lax.rsqrt(ms + EPS)) * coef2 + sh2).astype(BF16)

    subs = [slice(r0, r0 + POST_SUB) for r0 in range(0, TM, POST_SUB)]
    ys = []
    nxt = head(subs[0])
    for s, rows in enumerate(subs):
        x1, h2 = nxt
        a = jnp.maximum(_dot(h2, wff1_ref[...]), 0.0)
        if s + 1 < len(subs):
            nxt = head(subs[s + 1])
        x2 = x1 + ga2 * _dot((a * a).astype(BF16), wff2_ref[...])
        if final:
            ms2 = jnp.mean(x2 * x2, axis=-1, keepdims=True)
            x2 = (x2 * lax.rsqrt(ms2 + EPS)) * nfin_ref[...]
        ys.append((rows, x2))

    if not final:
        for rows, y in ys:
            out_refs[0][rows, :] = y
        return
    yc_ref, yl_ref = out_refs

    @pl.when(is_ctx)
    def _():
        for rows, y in ys:
            yc_ref[rows, :] = y

    @pl.when(jnp.logical_not(is_ctx))
    def _():
        for rows, y in ys:
            yl_ref[rows, :] = y


def _post_call(layer, final, xs, o_dir, rest, mod, o_norm, ws, bs_t, w_out, norm_ffn, w_ff1,
               w_ff2, norm_final):
    split = len(xs) == 2
    const = dict(pipeline_mode=pl.Buffered(1))
    if split:
        x_specs = [pl.BlockSpec((TM, D_MODEL), lambda i: (jnp.minimum(i, CTX_TILES - 1), 0)),
                   pl.BlockSpec((TM, D_MODEL), lambda i: (jnp.maximum(i - CTX_TILES, 0), 0))]
    else:
        x_specs = [pl.BlockSpec((TM, D_MODEL), lambda i: (i, 0))]
    if final:
        out_specs = [
            pl.BlockSpec((TM, D_MODEL), lambda i: (jnp.minimum(i, CTX_TILES - 1), 0)),
            pl.BlockSpec((TM, D_MODEL), lambda i: (jnp.maximum(i - CTX_TILES, 0), 0)),
        ]
        out_shape = [jax.ShapeDtypeStruct((N_CTX, D_MODEL), F32),
                     jax.ShapeDtypeStruct((N_LAT, D_MODEL), F32)]
    else:
        out_specs = pl.BlockSpec((TM, D_MODEL), lambda i: (i, 0))
        out_shape = jax.ShapeDtypeStruct((N_TOK, D_MODEL), F32)
    return pl.pallas_call(
        functools.partial(_post_kernel, final, split),
        grid=(N_TILES,),
        in_specs=x_specs + [
            pl.BlockSpec((None, TM, D_A), lambda i: (0, i, 0)),
            pl.BlockSpec((None, TM, D_A), lambda i: (1, i, 0)),
            pl.BlockSpec((TM, 3 * D_A), lambda i: (i, 0)),
            pl.BlockSpec((None, 1, 1, N_MOD * D_MODEL), lambda i: (layer, _mod_row(i), 0, 0)),
            pl.BlockSpec((None, 1, DV), lambda i: (layer, 0, 0), **const),
            pl.BlockSpec((None, G_B, CHUNK_MLP, CHUNK_MLP), lambda i: (layer, 0, 0, 0), **const),
            pl.BlockSpec((None, CHUNK_MLP, 128), lambda i: (layer, 0, 0), **const),
            pl.BlockSpec((None, D_MODEL, D_MODEL), lambda i: (layer, 0, 0), **const),
            pl.BlockSpec((None, 1, D_MODEL), lambda i: (layer, 0, 0), **const),
            pl.BlockSpec((None, D_MODEL, D_FF), lambda i: (layer, 0, 0), **const),
            pl.BlockSpec((None, D_FF, D_MODEL), lambda i: (layer, 0, 0), **const),
            pl.BlockSpec((1, D_MODEL), lambda i: (0, 0), **const),
        ],
        out_specs=out_specs,
        out_shape=out_shape,
        compiler_params=pltpu.CompilerParams(
            dimension_semantics=("arbitrary",), vmem_limit_bytes=VMEM_LIMIT_BYTES),
        name=f"post_l{layer}",
    )(*xs, o_dir, o_dir, rest, mod, o_norm, ws, bs_t, w_out, norm_ffn, w_ff1, w_ff2, norm_final)


def kernel(x_prompt, x_sample, state_delta, c, c_ctx, w_mod, b_mod, norm_mix, w_in, conv_qkv, a_log,
           dt_bias, o_norm, w_spatial, b_spatial, w_out, norm_ffn, w_ff1, w_ff2, norm_final):
    cond8 =jnp.concatenate([c_ctx[None, :], c, jnp.zeros((8 - 1 - LAT_B, D_MODEL), F32)], axis=0)

    n_qkvg = 4 * D_A
    w_main = jnp.concatenate([w_in[:, :, :n_qkvg], w_in[:, :, n_qkvg + 4 * H_A:]], axis=-1).astype(BF16)
    w_ab = jnp.pad(w_in[:, :, n_qkvg:n_qkvg + 4 * H_A], ((0, 0), (0, 0), (0, 128 - 4 * H_A))).astype(BF16)
    a_row = jnp.pad(a_log.reshape(DEPTH, 1, 2 * H_A), ((0, 0), (0, 0), (2 * H_A, 128 - 4 * H_A)))
    dt_row = jnp.pad(dt_bias.reshape(DEPTH, 1, 2 * H_A), ((0, 0), (0, 0), (2 * H_A, 128 - 4 * H_A)))
    bs_t = jnp.pad(jnp.swapaxes(b_spatial, 1, 2), ((0, 0), (0, 0), (0, 128 - G_B)))
    ws_bf = w_spatial.astype(BF16)
    w_out_bf = w_out.astype(BF16)
    w_ff1_bf = w_ff1.astype(BF16)
    w_ff2_bf = w_ff2.astype(BF16)
    norm_mix3 = norm_mix.reshape(DEPTH, 1, D_MODEL)
    norm_ffn3 = norm_ffn.reshape(DEPTH, 1, D_MODEL)
    o_norm3 = o_norm.reshape(DEPTH, 1, DV)
    norm_final2 = norm_final.reshape(1, D_MODEL)

    mod = _modulation(cond8, w_mod, b_mod).reshape(DEPTH, 8, 1, N_MOD * D_MODEL)

    xs = (x_prompt.reshape(N_CTX, D_MODEL), x_sample.reshape(N_LAT, D_MODEL))
    new_state = jnp.zeros((CTX_B, DEPTH, 2, H_A, DK, DV), F32)
    for layer in range(DEPTH):
        qkv, rest, gb = _pre_call(layer, xs, mod, norm_mix3, w_main, w_ab, conv_qkv, a_row, dt_row)
        o_dir, new_state = _delta_call(layer, qkv, gb, state_delta, new_state)
        out = _post_call(layer, layer == DEPTH - 1, xs, o_dir, rest, mod, o_norm3, ws_bf, bs_t,
                         w_out_bf, norm_ffn3, w_ff1_bf, w_ff2_bf, norm_final2)
        xs = (out,)

    y_prompt, y_sample = out
    return (y_prompt.reshape(CTX_B, CTX_T, D_MODEL), y_sample.reshape(LAT_B, LAT_T, D_MODEL), new_state)
```

```python
import functools

import jax
import jax.numpy as jnp
from jax import lax
from jax.experimental import pallas as pl
from jax.experimental.pallas import tpu as pltpu

F32 = jnp.float32
BF16 = jnp.bfloat16

D_MODEL = 1024
DEPTH = 4
N_MOD = 6
H_A = 4
DK = 128
DV = 128
D_A = H_A * DK
G_B = 4
C_B = 128
D_B = G_B * C_B
CHUNK_MLP = 128
D_FF = 4 * D_MODEL
EPS = 1e-6

CTX_B, CTX_T = 32, 256
LAT_B, LAT_T = 2, 4096
N_CTX = CTX_B * CTX_T
N_LAT = LAT_B * LAT_T
N_TOK = N_CTX + N_LAT

TM = 512
N_TILES = N_TOK // TM
CTX_TILES = N_CTX // TM
LAT_TILES_PER_SEQ = LAT_T // TM
HALO = 8
POST_SUB = 256

CHUNK = 128
INV_BASE = 8
TD = 512
CHUNKS_PER_TD = TD // CHUNK
SEQ_PER_CTX_DTILE = TD // CTX_T
CTX_DTILES = N_CTX // TD
LAT_DTILES_PER_SEQ = LAT_T // TD
N_DTILES = N_TOK // TD

N_FEAT = 6 * D_A
VMEM_LIMIT_BYTES = 60000 * 1024


def _sigmoid(x):
    return 1.0 / (1.0 + jnp.exp(-x))


def _silu(x):
    hx = 0.5 * x
    return hx + hx * jnp.tanh(hx)


_GELU_C0 = 0.7978845608028654
_GELU_C1 = _GELU_C0 * 0.044715


def _gelu_tanh(x):
    hx = 0.5 * x
    return hx + hx * jnp.tanh(x * (_GELU_C0 + _GELU_C1 * (x * x)))


def _softplus(x):
    return jnp.maximum(x, 0.0) + jnp.log1p(jnp.exp(-jnp.abs(x)))


def _dot(a, b):
    return jnp.dot(a, b, preferred_element_type=F32)


def _dot_nt(a, b):
    return lax.dot_general(a, b, (((1,), (1,)), ((), ())), preferred_element_type=F32)


def _dot_tn(a, b):
    return lax.dot_general(a, b, (((0,), (0,)), ((), ())), preferred_element_type=F32)


def _mod_kernel(cond_ref, w_ref, b_ref, o_ref):
    c = cond_ref[...]
    s = (c * _sigmoid(c)).astype(BF16)
    o_ref[0] = _dot(s, w_ref[0].astype(BF16)) + b_ref[0]


def _modulation(cond8, w_mod, b_mod):
    tn = 1536
    return pl.pallas_call(
        _mod_kernel,
        grid=(DEPTH, (N_MOD * D_MODEL) // tn),
        in_specs=[
            pl.BlockSpec((8, D_MODEL), lambda l, j: (0, 0)),
            pl.BlockSpec((1, D_MODEL, tn), lambda l, j: (l, 0, j)),
            pl.BlockSpec((1, 1, tn), lambda l, j: (l, 0, j)),
        ],
        out_specs=pl.BlockSpec((1, 8, tn), lambda l, j: (l, 0, j)),
        out_shape=jax.ShapeDtypeStruct((DEPTH, 8, N_MOD * D_MODEL), F32),
        compiler_params=pltpu.CompilerParams(
            dimension_semantics=("arbitrary", "arbitrary"), vmem_limit_bytes=VMEM_LIMIT_BYTES),
        name="modulation",
    )(cond8, w_mod, b_mod.reshape(DEPTH, 1, N_MOD * D_MODEL))


def _mod_row(i):
    return jnp.where(i < CTX_TILES, 0, 1 + (i - CTX_TILES) // LAT_TILES_PER_SEQ)


def _pre_kernel(split, *refs):
    i = pl.program_id(0)
    is_ctx = i < CTX_TILES
    if split:
        (xc_ref, xl_ref, xcp_ref, xlp_ref, xcn_ref, xln_ref, mod_ref, nw_ref, w_ref, wab_ref,
         conv_ref, arow_ref, dtrow_ref, qkv_ref, rest_ref, gb_ref) = refs
        x = jnp.where(is_ctx, xc_ref[...], xl_ref[...])
        x_prev = jnp.where(is_ctx, xcp_ref[...], xlp_ref[...])
        x_next = jnp.where(is_ctx, xcn_ref[...], xln_ref[...])
    else:
        (x_ref, xp_ref, xn_ref, mod_ref, nw_ref, w_ref, wab_ref,
         conv_ref, arow_ref, dtrow_ref, qkv_ref, rest_ref, gb_ref) = refs
        x, x_prev, x_next = x_ref[...], xp_ref[...], xn_ref[...]
    m = mod_ref[0]
    sh1 = m[:, 0:D_MODEL]
    coef = nw_ref[...] * (1.0 + m[:, D_MODEL:2 * D_MODEL])

    def norm_mod(xx):
        ms = jnp.mean(xx * xx, axis=-1, keepdims=True)
        return ((xx * lax.rsqrt(ms + EPS)) * coef + sh1).astype(BF16)

    h = norm_mod(x)
    hh = norm_mod(jnp.concatenate([x_prev, x_next], axis=0))
    h_ext = jnp.concatenate([h, hh], axis=0)

    lat_j = (i - CTX_TILES) % LAT_TILES_PER_SEQ
    starts_seq = jnp.logical_or(is_ctx, lat_j == 0)
    ends_seq = jnp.logical_or(is_ctx, lat_j == LAT_TILES_PER_SEQ - 1)
    r8 = lax.broadcasted_iota(jnp.int32, (HALO, 1), 0)

    def shifted(p, prev_row, next_row):
        dn = pltpu.roll(p, 1, 0)
        up = pltpu.roll(p, TM - 1, 0)
        dn_parts, up_parts = [], []
        for b in range(0, TM, CTX_T):
            slab = dn[b:b + HALO]
            if b == 0:
                slab = jnp.where(r8 == 0, jnp.where(starts_seq, 0.0, prev_row), slab)
            else:
                slab = jnp.where(jnp.logical_and(r8 == 0, is_ctx), 0.0, slab)
            dn_parts += [slab, dn[b + HALO:b + CTX_T]]
            e = b + CTX_T
            slab = up[e - HALO:e]
            if e == TM:
                slab = jnp.where(r8 == HALO - 1, jnp.where(ends_seq, 0.0, next_row), slab)
            else:
                slab = jnp.where(jnp.logical_and(r8 == HALO - 1, is_ctx), 0.0, slab)
            up_parts += [up[b:e - HALO], slab]
        return jnp.concatenate(dn_parts, axis=0), jnp.concatenate(up_parts, axis=0)

    cw = conv_ref[...]
    for blk in range(3):
        cols = slice(blk * D_A, (blk + 1) * D_A)
        pe = _dot(h_ext, w_ref[:, cols])
        p = pe[0:TM]
        xm1, xp1 = shifted(p, pe[TM + HALO - 1:TM + HALO], pe[TM + HALO:TM + HALO + 1])
        y = _silu(cw[0:1, cols] * xm1 + cw[1:2, cols] * p + cw[2:3, cols] * xp1)
        if blk < 2:
            scale = DK ** -0.5 if blk == 0 else 1.0
            for hd in range(H_A):
                yh = y[:, hd * DK:(hd + 1) * DK]
                inv_norm = lax.rsqrt(jnp.sum(yh * yh, axis=-1, keepdims=True) + EPS) * scale
                qkv_ref[:, blk * D_A + hd * DK:blk * D_A + (hd + 1) * DK] = (yh * inv_norm).astype(BF16)
        else:
            qkv_ref[:, cols] = y.astype(BF16)

    rest_ref[...] = _dot(h, w_ref[:, 3 * D_A:6 * D_A]).astype(BF16)

    ab = _dot(h, wab_ref[...])
    lane = lax.broadcasted_iota(jnp.int32, (TM, 128), 1)
    beta = _sigmoid(ab)
    g_log = -jnp.exp(arow_ref[...]) * _softplus(ab + dtrow_ref[...])
    gb_ref[...] = jnp.where(lane < 2 * H_A, beta, g_log)


def _pre_call(layer, xs, mod, norm_mix, w_main, w_ab, conv_qkv, a_row, dt_row):
    split = len(xs) == 2
    bpt = TM // HALO
    const = dict(pipeline_mode=pl.Buffered(1))

    def tile_specs(first_tile, n_tiles):
        def local(i):
            return jnp.clip(i - first_tile, 0, n_tiles - 1)
        return (
            pl.BlockSpec((TM, D_MODEL), lambda i: (local(i), 0)),
            pl.BlockSpec((HALO, D_MODEL), lambda i: (jnp.maximum(local(i) * bpt - 1, 0), 0)),
            pl.BlockSpec((HALO, D_MODEL),
                         lambda i: (jnp.minimum((local(i) + 1) * bpt, n_tiles * bpt - 1), 0)),
        )

    if split:
        c_specs = tile_specs(0, CTX_TILES)
        l_specs = tile_specs(CTX_TILES, N_TILES - CTX_TILES)
        x_specs = [c_specs[0], l_specs[0], c_specs[1], l_specs[1], c_specs[2], l_specs[2]]
        x_args = [xs[0], xs[1]] * 3
    else:
        x_specs = list(tile_specs(0, N_TILES))
        x_args = [xs[0]] * 3
    return pl.pallas_call(
        functools.partial(_pre_kernel, split),
        grid=(N_TILES,),
        in_specs=x_specs + [
            pl.BlockSpec((None, 1, 1, N_MOD * D_MODEL), lambda i: (layer, _mod_row(i), 0, 0)),
            pl.BlockSpec((None, 1, D_MODEL), lambda i: (layer, 0, 0), **const),
            pl.BlockSpec((None, D_MODEL, N_FEAT), lambda i: (layer, 0, 0), **const),
            pl.BlockSpec((None, D_MODEL, 128), lambda i: (layer, 0, 0), **const),
            pl.BlockSpec((None, 3, 3 * D_A), lambda i: (layer, 0, 0), **const),
            pl.BlockSpec((None, 1, 128), lambda i: (layer, 0, 0), **const),
            pl.BlockSpec((None, 1, 128), lambda i: (layer, 0, 0), **const),
        ],
        out_specs=[
            pl.BlockSpec((TM, 3 * D_A), lambda i: (i, 0)),
            pl.BlockSpec((TM, 3 * D_A), lambda i: (i, 0)),
            pl.BlockSpec((TM, 128), lambda i: (i, 0)),
        ],
        out_shape=[
            jax.ShapeDtypeStruct((N_TOK, 3 * D_A), BF16),
            jax.ShapeDtypeStruct((N_TOK, 3 * D_A), BF16),
            jax.ShapeDtypeStruct((N_TOK, 128), F32),
        ],
        compiler_params=pltpu.CompilerParams(
            dimension_semantics=("arbitrary",), vmem_limit_bytes=VMEM_LIMIT_BYTES),
        name=f"pre_l{layer}",
    )(*x_args, mod, norm_mix, w_main, w_ab, conv_qkv, a_row, dt_row)


def _delta_tile_row(rev, t):
    if not rev:
        return t
    lat = t - CTX_DTILES
    b = lat // LAT_DTILES_PER_SEQ
    j = LAT_DTILES_PER_SEQ - 1 - lat % LAT_DTILES_PER_SEQ
    return jnp.where(t < CTX_DTILES, t, CTX_DTILES + b * LAT_DTILES_PER_SEQ + j)


def _mm_inv(a, b):
    return _dot(a.astype(BF16), b.astype(BF16))


def _row_groups(x, size):
    return [x[g * size:(g + 1) * size] for g in range(x.shape[0] // size)]


def _delta_direction(rev, is_ctx, q_ref, k_ref, v_ref, gb_ref, o_ref, state0):
    row = lax.broadcasted_iota(jnp.int32, (CHUNK, CHUNK), 0)
    col = lax.broadcasted_iota(jnp.int32, (CHUNK, CHUNK), 1)
    incl = row <= col if rev else row >= col
    strict = row < col if rev else row > col
    eye = (row == col).astype(F32)
    base_mask = (row // INV_BASE) == (col // INV_BASE)
    levels = []
    blk = INV_BASE
    while blk < CHUNK:
        levels.append((blk, jnp.logical_and((row // (2 * blk)) == (col // (2 * blk)),
                                            (row // blk) != (col // blk))))
        blk *= 2
    rix = lax.broadcasted_iota(jnp.int32, (CHUNK, 128), 0)
    lane_beta = H_A if rev else 0
    lane_g = 3 * H_A if rev else 2 * H_A

    chunk_order = range(CHUNKS_PER_TD - 1, -1, -1) if rev else range(CHUNKS_PER_TD)
    chunk_rows = [slice(cc * CHUNK, (cc + 1) * CHUNK) for cc in chunk_order]
    probs = [[dict(hd=hd, hcols=slice(hd * DK, (hd + 1) * DK)) for hd in range(H_A)]
             for _ in range(CHUNKS_PER_TD)]
    carry = dict(state=list(state0), mid=None)

    def prep(c):
        rows = chunk_rows[c]
        gb = gb_ref[rows, :]
        fwd_sum = gb
        s = 1
        while s < CHUNK:
            fwd_sum = fwd_sum + jnp.where(rix >= s, pltpu.roll(fwd_sum, s, 0), 0.0)
            s *= 2
        total = fwd_sum[CHUNK - 1:CHUNK, :]
        cum = total - fwd_sum + gb if rev else fwd_sum
        for p in probs[c]:
            hd, hcols = p["hd"], p["hcols"]
            beta = gb[:, lane_beta + hd:lane_beta + hd + 1]
            g_c = cum[:, lane_g + hd:lane_g + hd + 1]
            g_tot = total[:, lane_g + hd:lane_g + hd + 1]
            g_b = jnp.broadcast_to(g_c, (CHUNK, CHUNK))
            eg = jnp.exp(g_c)
            q_bf = q_ref[rows, hcols]
            k_bf = k_ref[rows, hcols]
            k = k_bf.astype(F32)
            kb = k * beta
            p.update(
                decay=jnp.where(incl, jnp.exp(jnp.where(incl, g_b - g_b.T, 0.0)), 0.0),
                k_bf=k_bf,
                kq_lhs=jnp.concatenate([kb.astype(BF16), q_bf], axis=0),
                rhs=jnp.concatenate([v_ref[rows, hcols].astype(F32) * beta, kb * eg],
                                    axis=-1).astype(BF16),
                q_dec=(q_bf.astype(F32) * eg).astype(BF16),
                k_dec=(k * jnp.exp(g_tot - g_c)).astype(BF16),
                g_tot=g_tot)

    def kq_stage(c):
        for p in probs[c]:
            kq = _dot_nt(p["kq_lhs"], p["k_bf"])
            p["a"] = jnp.where(strict, kq[0:CHUNK] * p["decay"], 0.0)
            p["qk"] = jnp.where(incl, kq[CHUNK:2 * CHUNK] * p["decay"], 0.0).astype(BF16)

    def base_first(c):
        for p in probs[c]:
            a0 = jnp.where(base_mask, p["a"], 0.0)
            p["inv"] = eye - a0
            p["pw"] = _mm_inv(a0, a0)

    def base_mid(c):
        for p in probs[c]:
            st = _mm_inv(jnp.concatenate([p["pw"], p["inv"]], axis=0), p["pw"])
            p["inv"] = p["inv"] + st[CHUNK:2 * CHUNK]
            p["pw"] = st[0:CHUNK]

    def base_last(c):
        for p in probs[c]:
            p["inv"] = p["inv"] + _mm_inv(p["inv"], p["pw"])

    upd_parity = 0 if rev else 1

    def merge_first(size, off_mask, c):
        for p in probs[c]:
            p["groups"] = _row_groups(p["inv"], size)
            lhs = jnp.concatenate(p["groups"][upd_parity::2], axis=0)
            p["tmp"] = _mm_inv(lhs, jnp.where(off_mask, p["a"], 0.0))

    def merge_second(size, c):
        for p in probs[c]:
            upd = _row_groups(_mm_inv(p["tmp"], p["inv"]), size)
            new = [g - upd[i // 2] if i % 2 == upd_parity else g for i, g in enumerate(p["groups"])]
            p["inv"] = jnp.concatenate(new, axis=0)

    def solve(c):
        for p in probs[c]:
            sol = _dot(p["inv"].astype(BF16), p["rhs"])
            p["u"] = sol[:, 0:DV]
            p["w"] = sol[:, DV:2 * DV].astype(BF16)

    def rec_first(c):
        if c == CHUNKS_PER_TD // 2:
            carry["mid"] = carry["state"]
            carry["state"] = [jnp.where(is_ctx, 0.0, st) for st in carry["state"]]
        for p in probs[c]:
            p["st_bf"] = carry["state"][p["hd"]].astype(BF16)
            p["ws"] = _dot(p["w"], p["st_bf"])

    def rec_second(c):
        state = list(carry["state"])
        for p in probs[c]:
            v_new = (p["u"] - p["ws"]).astype(BF16)
            o_ref[chunk_rows[c], p["hcols"]] = _dot(jnp.concatenate([p["q_dec"], p["qk"]], axis=1),
                                                    jnp.concatenate([p["st_bf"], v_new], axis=0))
            state[p["hd"]] = state[p["hd"]] * jnp.exp(p["g_tot"]) + _dot_tn(p["k_dec"], v_new)
        carry["state"] = state

    stages = [prep, kq_stage, base_first]
    n = 2
    while 2 * n < INV_BASE:
        stages.append(base_mid)
        n *= 2
    stages.append(base_last)
    for size, off_mask in levels:
        stages.append(functools.partial(merge_first, size, off_mask))
        stages.append(functools.partial(merge_second, size))
    stages.append(solve)
    return stages, rec_first, rec_second, carry


def _delta_kernel(qf_ref, kf_ref, vf_ref, gbf_ref, qb_ref, kb_ref, vb_ref, gbb_ref, s0_ref,
                  ns_in_ref, of_ref, ob_ref, ns_ref, s_scr):
    del ns_in_ref
    t = pl.program_id(0)
    is_ctx = t < CTX_DTILES

    @pl.when(is_ctx)
    def _():
        s_scr[...] = jnp.zeros_like(s_scr)

    @pl.when(jnp.logical_and(t >= CTX_DTILES, (t - CTX_DTILES) % LAT_DTILES_PER_SEQ == 0))
    def _():
        s_scr[...] = s0_ref[...]

    dirs = [
        _delta_direction(False, is_ctx, qf_ref, kf_ref, vf_ref, gbf_ref, of_ref,
                         [s_scr[0, hd] for hd in range(H_A)]),
        _delta_direction(True, is_ctx, qb_ref, kb_ref, vb_ref, gbb_ref, ob_ref,
                         [s_scr[1, hd] for hd in range(H_A)]),
    ]
    for k in range(len(dirs[0][0])):
        for stages, _, _, _ in dirs:
            for c in range(CHUNKS_PER_TD):
                stages[k](c)
    for c in range(CHUNKS_PER_TD):
        for _, rec_first, _, _ in dirs:
            rec_first(c)
        for _, _, rec_second, _ in dirs:
            rec_second(c)

    for d, (_, _, _, carry) in enumerate(dirs):
        for hd in range(H_A):
            s_scr[d, hd] = carry["state"][hd]

    @pl.when(is_ctx)
    def _():
        for d, (_, _, _, carry) in enumerate(dirs):
            first, second = (1, 0) if d == 1 else (0, 1)
            for hd in range(H_A):
                ns_ref[first, d, hd] = carry["mid"][hd]
                ns_ref[second, d, hd] = carry["state"][hd]


def _delta_call(layer, qkv, gb, state_delta, new_state):
    def tile_specs(rev):
        def spec(width, col_block):
            return pl.BlockSpec((TD, width), lambda t: (_delta_tile_row(rev, t), col_block))
        return [spec(D_A, 0), spec(D_A, 1), spec(D_A, 2), spec(128, 0)]

    def s0_index(t):
        b = jnp.maximum(t - CTX_DTILES, 0) // LAT_DTILES_PER_SEQ
        return (b, layer, 0, 0, 0, 0)

    def ns_index(t):
        return (jnp.minimum(t, CTX_DTILES - 1), layer, 0, 0, 0, 0)

    return pl.pallas_call(
        _delta_kernel,
        grid=(N_DTILES,),
        in_specs=tile_specs(False) + tile_specs(True) + [
            pl.BlockSpec((None, None, 2, H_A, DK, DV), s0_index),
            pl.BlockSpec(memory_space=pl.ANY),
        ],
        out_specs=[
            pl.BlockSpec((TD, D_A), lambda t: (_delta_tile_row(False, t), 0)),
            pl.BlockSpec((TD, D_A), lambda t: (_delta_tile_row(True, t), 0)),
            pl.BlockSpec((SEQ_PER_CTX_DTILE, None, 2, H_A, DK, DV), ns_index),
        ],
        out_shape=[
            jax.ShapeDtypeStruct((N_TOK, D_A), F32),
            jax.ShapeDtypeStruct((N_TOK, D_A), F32),
            jax.ShapeDtypeStruct(new_state.shape, F32),
        ],
        scratch_shapes=[pltpu.VMEM((2, H_A, DK, DV), F32)],
        input_output_aliases={9: 2},
        compiler_params=pltpu.CompilerParams(
            dimension_semantics=("arbitrary",), vmem_limit_bytes=VMEM_LIMIT_BYTES),
        name=f"delta_l{layer}",
    )(qkv, qkv, qkv, gb, qkv, qkv, qkv, gb, state_delta, new_state)


def _post_kernel(final, split, *refs):
    is_ctx = pl.program_id(0) < CTX_TILES
    (of_ref, ob_ref, rest_ref, mod_ref, onorm_ref, ws_ref, bs_ref, wout_ref, nffn_ref, wff1_ref,
     wff2_ref, nfin_ref) = refs[1 + split:13 + split]
    out_refs = refs[13 + split:]
    m = mod_ref[0]
    ga1 = m[:, 2 * D_MODEL:3 * D_MODEL]
    sh2 = m[:, 3 * D_MODEL:4 * D_MODEL]
    coef2 = nffn_ref[...] * (1.0 + m[:, 4 * D_MODEL:5 * D_MODEL])
    ga2 = m[:, 5 * D_MODEL:6 * D_MODEL]
    onw = onorm_ref[...]
    bs = bs_ref[...]

    def head(rows):
        if split:
            x = jnp.where(is_ctx, refs[0][rows, :], refs[1][rows, :])
        else:
            x = refs[0][rows, :]
        o = of_ref[rows, :] + ob_ref[rows, :]
        mix_in = []
        for hd in range(H_A):
            oh = o[:, hd * DV:(hd + 1) * DV]
            oh = oh * (lax.rsqrt(jnp.mean(oh * oh, axis=-1, keepdims=True) + EPS) * onw)
            gate = _silu(rest_ref[rows, hd * DV:(hd + 1) * DV].astype(F32))
            mix_in.append((oh * gate).astype(BF16))
        for g in range(G_B):
            ug = _gelu_tanh(rest_ref[rows, D_A + g * C_B:D_A + (g + 1) * C_B].astype(F32))
            vg = _gelu_tanh(rest_ref[rows, 2 * D_A + g * C_B:2 * D_A + (g + 1) * C_B].astype(F32))
            dv = vg - jnp.mean(vg, axis=-1, keepdims=True)
            var = jnp.mean(dv * dv, axis=-1, keepdims=True)
            vgn = (dv * lax.rsqrt(var + EPS)).astype(BF16)
            parts = []
            for c in range(POST_SUB // CHUNK_MLP):
                crows = slice(c * CHUNK_MLP, (c + 1) * CHUNK_MLP)
                mixed = _dot(ws_ref[g], vgn[crows]) + bs[:, g:g + 1]
                parts.append((ug[crows] * mixed).astype(BF16))
            mix_in.append(jnp.concatenate(parts, axis=0))
        mix = jnp.concatenate(mix_in, axis=-1)
        x1 = x + ga1 * _dot(mix, wout_ref[...])
        ms = jnp.mean(x1 * x1, axis=-1, keepdims=True)
        return x1, ((x1 * lax.rsqrt(ms + EPS)) * coef2 + sh2).astype(BF16)

    subs = [slice(r0, r0 + POST_SUB) for r0 in range(0, TM, POST_SUB)]
    ys = []
    nxt = head(subs[0])
    for s, rows in enumerate(subs):
        x1, h2 = nxt
        a = jnp.maximum(_dot(h2, wff1_ref[...]), 0.0)
        if s + 1 < len(subs):
            nxt = head(subs[s + 1])
        x2 = x1 + ga2 * _dot((a * a).astype(BF16), wff2_ref[...])
        if final:
            ms2 = jnp.mean(x2 * x2, axis=-1, keepdims=True)
            x2 = (x2 * lax.rsqrt(ms2 + EPS)) * nfin_ref[...]
        ys.append((rows, x2))

    if not final:
        for rows, y in ys:
            out_refs[0][rows, :] = y
        return
    yc_ref, yl_ref = out_refs

    @pl.when(is_ctx)
    def _():
        for rows, y in ys:
            yc_ref[rows, :] = y

    @pl.when(jnp.logical_not(is_ctx))
    def _():
        for rows, y in ys:
            yl_ref[rows, :] = y


def _post_call(layer, final, xs, o_fwd, o_bwd, rest, mod, o_norm, ws, bs_t, w_out, norm_ffn, w_ff1,
               w_ff2, norm_final):
    split = len(xs) == 2
    const = dict(pipeline_mode=pl.Buffered(1))
    if split:
        x_specs = [pl.BlockSpec((TM, D_MODEL), lambda i: (jnp.minimum(i, CTX_TILES - 1), 0)),
                   pl.BlockSpec((TM, D_MODEL), lambda i: (jnp.maximum(i - CTX_TILES, 0), 0))]
    else:
        x_specs = [pl.BlockSpec((TM, D_MODEL), lambda i: (i, 0))]
    if final:
        out_specs = [
            pl.BlockSpec((TM, D_MODEL), lambda i: (jnp.minimum(i, CTX_TILES - 1), 0)),
            pl.BlockSpec((TM, D_MODEL), lambda i: (jnp.maximum(i - CTX_TILES, 0), 0)),
        ]
        out_shape = [jax.ShapeDtypeStruct((N_CTX, D_MODEL), F32),
                     jax.ShapeDtypeStruct((N_LAT, D_MODEL), F32)]
    else:
        out_specs = pl.BlockSpec((TM, D_MODEL), lambda i: (i, 0))
        out_shape = jax.ShapeDtypeStruct((N_TOK, D_MODEL), F32)
    return pl.pallas_call(
        functools.partial(_post_kernel, final, split),
        grid=(N_TILES,),
        in_specs=x_specs + [
            pl.BlockSpec((TM, D_A), lambda i: (i, 0)),
            pl.BlockSpec((TM, D_A), lambda i: (i, 0)),
            pl.BlockSpec((TM, 3 * D_A), lambda i: (i, 0)),
            pl.BlockSpec((None, 1, 1, N_MOD * D_MODEL), lambda i: (layer, _mod_row(i), 0, 0)),
            pl.BlockSpec((None, 1, DV), lambda i: (layer, 0, 0), **const),
            pl.BlockSpec((None, G_B, CHUNK_MLP, CHUNK_MLP), lambda i: (layer, 0, 0, 0), **const),
            pl.BlockSpec((None, CHUNK_MLP, 128), lambda i: (layer, 0, 0), **const),
            pl.BlockSpec((None, D_MODEL, D_MODEL), lambda i: (layer, 0, 0), **const),
            pl.BlockSpec((None, 1, D_MODEL), lambda i: (layer, 0, 0), **const),
            pl.BlockSpec((None, D_MODEL, D_FF), lambda i: (layer, 0, 0), **const),
            pl.BlockSpec((None, D_FF, D_MODEL), lambda i: (layer, 0, 0), **const),
            pl.BlockSpec((1, D_MODEL), lambda i: (0, 0), **const),
        ],
        out_specs=out_specs,
        out_shape=out_shape,
        compiler_params=pltpu.CompilerParams(
            dimension_semantics=("arbitrary",), vmem_limit_bytes=VMEM_LIMIT_BYTES),
        name=f"post_l{layer}",
    )(*xs, o_fwd, o_bwd, rest, mod, o_norm, ws, bs_t, w_out, norm_ffn, w_ff1, w_ff2, norm_final)


def kernel(x_prompt, x_sample, state_delta, c, c_ctx, w_mod, b_mod, norm_mix, w_in, conv_qkv, a_log,
           dt_bias, o_norm, w_spatial, b_spatial, w_out, norm_ffn, w_ff1, w_ff2, norm_final):
    cond8 =jnp.concatenate([c_ctx[None, :], c, jnp.zeros((8 - 1 - LAT_B, D_MODEL), F32)], axis=0)

    n_qkvg = 4 * D_A
    w_main = jnp.concatenate([w_in[:, :, :n_qkvg], w_in[:, :, n_qkvg + 4 * H_A:]], axis=-1).astype(BF16)
    w_ab = jnp.pad(w_in[:, :, n_qkvg:n_qkvg + 4 * H_A], ((0, 0), (0, 0), (0, 128 - 4 * H_A))).astype(BF16)
    a_row = jnp.pad(a_log.reshape(DEPTH, 1, 2 * H_A), ((0, 0), (0, 0), (2 * H_A, 128 - 4 * H_A)))
    dt_row = jnp.pad(dt_bias.reshape(DEPTH, 1, 2 * H_A), ((0, 0), (0, 0), (2 * H_A, 128 - 4 * H_A)))
    bs_t = jnp.pad(jnp.swapaxes(b_spatial, 1, 2), ((0, 0), (0, 0), (0, 128 - G_B)))
    ws_bf = w_spatial.astype(BF16)
    w_out_bf = w_out.astype(BF16)
    w_ff1_bf = w_ff1.astype(BF16)
    w_ff2_bf = w_ff2.astype(BF16)
    norm_mix3 = norm_mix.reshape(DEPTH, 1, D_MODEL)
    norm_ffn3 = norm_ffn.reshape(DEPTH, 1, D_MODEL)
    o_norm3 = o_norm.reshape(DEPTH, 1, DV)
    norm_final2 = norm_final.reshape(1, D_MODEL)

    mod = _modulation(cond8, w_mod, b_mod).reshape(DEPTH, 8, 1, N_MOD * D_MODEL)

    xs = (x_prompt.reshape(N_CTX, D_MODEL), x_sample.reshape(N_LAT, D_MODEL))
    new_state = jnp.zeros((CTX_B, DEPTH, 2, H_A, DK, DV), F32)
    for layer in range(DEPTH):
        qkv, rest, gb = _pre_call(layer, xs, mod, norm_mix3, w_main, w_ab, conv_qkv, a_row, dt_row)
        o_fwd, o_bwd, new_state = _delta_call(layer, qkv, gb, state_delta, new_state)
        out = _post_call(layer, layer == DEPTH - 1, xs, o_fwd, o_bwd, rest, mod, o_norm3, ws_bf, bs_t,
                         w_out_bf, norm_ffn3, w_ff1_bf, w_ff2_bf, norm_final2)
        xs = (out,)

    y_prompt, y_sample = out
    return (y_prompt.reshape(CTX_B, CTX_T, D_MODEL), y_sample.reshape(LAT_B, LAT_T, D_MODEL), new_state)
```

```python
import functools

import jax
import jax.numpy as jnp
from jax import lax
from jax.experimental import pallas as pl
from jax.experimental.pallas import tpu as pltpu

F32 = jnp.float32
BF16 = jnp.bfloat16

D_MODEL = 1024
DEPTH = 4
N_MOD = 6
H_A = 4
DK = 128
DV = 128
D_A = H_A * DK
G_B = 4
C_B = 128
D_B = G_B * C_B
CHUNK_MLP = 128
D_FF = 4 * D_MODEL
EPS = 1e-6

CTX_B, CTX_T = 32, 256
LAT_B, LAT_T = 2, 4096
N_CTX = CTX_B * CTX_T
N_LAT = LAT_B * LAT_T
N_TOK = N_CTX + N_LAT

TM = 512
N_TILES = N_TOK // TM
CTX_TILES = N_CTX // TM
LAT_TILES_PER_SEQ = LAT_T // TM
HALO = 8
POST_SUB = 256

CHUNK = 128
INV_BASE = 8
BF16_TILE_ROWS = 16
TD = 512
CHUNKS_PER_TD = TD // CHUNK
SEQ_PER_CTX_DTILE = TD // CTX_T
CTX_DTILES = N_CTX // TD
LAT_DTILES_PER_SEQ = LAT_T // TD
N_DTILES = N_TOK // TD

N_FEAT = 6 * D_A
VMEM_LIMIT_BYTES = 60000 * 1024


def _sigmoid(x):
    return 1.0 / (1.0 + jnp.exp(-x))


def _silu(x):
    hx = 0.5 * x
    return hx + hx * jnp.tanh(hx)


_GELU_C0 = 0.7978845608028654
_GELU_C1 = _GELU_C0 * 0.044715


def _gelu_tanh(x):
    hx = 0.5 * x
    return hx + hx * jnp.tanh(x * (_GELU_C0 + _GELU_C1 * (x * x)))


def _softplus(x):
    return jnp.maximum(x, 0.0) + jnp.log1p(jnp.exp(-jnp.abs(x)))


def _dot(a, b):
    return jnp.dot(a, b, preferred_element_type=F32)


def _dot_nt(a, b):
    return lax.dot_general(a, b, (((1,), (1,)), ((), ())), preferred_element_type=F32)


def _dot_tn(a, b):
    return lax.dot_general(a, b, (((0,), (0,)), ((), ())), preferred_element_type=F32)


def _mod_kernel(cond_ref, w_ref, b_ref, o_ref):
    c = cond_ref[...]
    s = (c * _sigmoid(c)).astype(BF16)
    o_ref[0] = _dot(s, w_ref[0].astype(BF16)) + b_ref[0]


def _modulation(cond8, w_mod, b_mod):
    tn = 1536
    return pl.pallas_call(
        _mod_kernel,
        grid=(DEPTH, (N_MOD * D_MODEL) // tn),
        in_specs=[
            pl.BlockSpec((8, D_MODEL), lambda l, j: (0, 0)),
            pl.BlockSpec((1, D_MODEL, tn), lambda l, j: (l, 0, j)),
            pl.BlockSpec((1, 1, tn), lambda l, j: (l, 0, j)),
        ],
        out_specs=pl.BlockSpec((1, 8, tn), lambda l, j: (l, 0, j)),
        out_shape=jax.ShapeDtypeStruct((DEPTH, 8, N_MOD * D_MODEL), F32),
        compiler_params=pltpu.CompilerParams(
            dimension_semantics=("arbitrary", "arbitrary"), vmem_limit_bytes=VMEM_LIMIT_BYTES),
        name="modulation",
    )(cond8, w_mod, b_mod.reshape(DEPTH, 1, N_MOD * D_MODEL))


def _mod_row(i):
    return jnp.where(i < CTX_TILES, 0, 1 + (i - CTX_TILES) // LAT_TILES_PER_SEQ)


def _pre_kernel(split, *refs):
    i = pl.program_id(0)
    is_ctx = i < CTX_TILES
    if split:
        (xc_ref, xl_ref, xcp_ref, xlp_ref, xcn_ref, xln_ref, mod_ref, nw_ref, w_ref, wab_ref,
         conv_ref, arow_ref, dtrow_ref, qkv_ref, rest_ref, gb_ref) = refs
        x = jnp.where(is_ctx, xc_ref[...], xl_ref[...])
        x_prev = jnp.where(is_ctx, xcp_ref[...], xlp_ref[...])
        x_next = jnp.where(is_ctx, xcn_ref[...], xln_ref[...])
    else:
        (x_ref, xp_ref, xn_ref, mod_ref, nw_ref, w_ref, wab_ref,
         conv_ref, arow_ref, dtrow_ref, qkv_ref, rest_ref, gb_ref) = refs
        x, x_prev, x_next = x_ref[...], xp_ref[...], xn_ref[...]
    m = mod_ref[0]
    sh1 = m[:, 0:D_MODEL]
    coef = nw_ref[...] * (1.0 + m[:, D_MODEL:2 * D_MODEL])

    def norm_mod(xx):
        ms = jnp.mean(xx * xx, axis=-1, keepdims=True)
        return ((xx * lax.rsqrt(ms + EPS)) * coef + sh1).astype(BF16)

    h = norm_mod(x)
    hh = norm_mod(jnp.concatenate([x_prev, x_next], axis=0))
    h_ext = jnp.concatenate([h, hh], axis=0)

    lat_j = (i - CTX_TILES) % LAT_TILES_PER_SEQ
    starts_seq = jnp.logical_or(is_ctx, lat_j == 0)
    ends_seq = jnp.logical_or(is_ctx, lat_j == LAT_TILES_PER_SEQ - 1)
    r8 = lax.broadcasted_iota(jnp.int32, (HALO, 1), 0)

    def shifted(p, prev_row, next_row):
        dn = pltpu.roll(p, 1, 0)
        up = pltpu.roll(p, TM - 1, 0)
        dn_parts, up_parts = [], []
        for b in range(0, TM, CTX_T):
            slab = dn[b:b + HALO]
            if b == 0:
                slab = jnp.where(r8 == 0, jnp.where(starts_seq, 0.0, prev_row), slab)
            else:
                slab = jnp.where(jnp.logical_and(r8 == 0, is_ctx), 0.0, slab)
            dn_parts += [slab, dn[b + HALO:b + CTX_T]]
            e = b + CTX_T
            slab = up[e - HALO:e]
            if e == TM:
                slab = jnp.where(r8 == HALO - 1, jnp.where(ends_seq, 0.0, next_row), slab)
            else:
                slab = jnp.where(jnp.logical_and(r8 == HALO - 1, is_ctx), 0.0, slab)
            up_parts += [up[b:e - HALO], slab]
        return jnp.concatenate(dn_parts, axis=0), jnp.concatenate(up_parts, axis=0)

    cw = conv_ref[...]
    for blk in range(3):
        cols = slice(blk * D_A, (blk + 1) * D_A)
        pe = _dot(h_ext, w_ref[:, cols])
        rcols = slice((3 + blk) * D_A, (4 + blk) * D_A)
        rest_ref[:, blk * D_A:(blk + 1) * D_A] = _dot(h, w_ref[:, rcols]).astype(BF16)
        if blk == 0:
            ab = _dot(h, wab_ref[...])
            lane = lax.broadcasted_iota(jnp.int32, (TM, 128), 1)
            g_log = -jnp.exp(arow_ref[...]) * _softplus(ab + dtrow_ref[...])
            gb_ref[...] = jnp.where(lane < 2 * H_A, _sigmoid(ab), g_log)
        p = pe[0:TM]
        xm1, xp1 = shifted(p, pe[TM + HALO - 1:TM + HALO], pe[TM + HALO:TM + HALO + 1])
        y = _silu(cw[0:1, cols] * xm1 + cw[1:2, cols] * p + cw[2:3, cols] * xp1)
        if blk < 2:
            scale = DK ** -0.5 if blk == 0 else 1.0
            for hd in range(H_A):
                yh = y[:, hd * DK:(hd + 1) * DK]
                inv_norm = lax.rsqrt(jnp.sum(yh * yh, axis=-1, keepdims=True) + EPS) * scale
                qkv_ref[:, blk * D_A + hd * DK:blk * D_A + (hd + 1) * DK] = (yh * inv_norm).astype(BF16)
        else:
            qkv_ref[:, cols] = y.astype(BF16)


def _pre_call(layer, xs, mod, norm_mix, w_main, w_ab, conv_qkv, a_row, dt_row):
    split = len(xs) == 2
    bpt = TM // HALO
    const = dict(pipeline_mode=pl.Buffered(1))

    def tile_specs(first_tile, n_tiles):
        def local(i):
            return jnp.clip(i - first_tile, 0, n_tiles - 1)
        return (
            pl.BlockSpec((TM, D_MODEL), lambda i: (local(i), 0)),
            pl.BlockSpec((HALO, D_MODEL), lambda i: (jnp.maximum(local(i) * bpt - 1, 0), 0)),
            pl.BlockSpec((HALO, D_MODEL),
                         lambda i: (jnp.minimum((local(i) + 1) * bpt, n_tiles * bpt - 1), 0)),
        )

    if split:
        c_specs = tile_specs(0, CTX_TILES)
        l_specs = tile_specs(CTX_TILES, N_TILES - CTX_TILES)
        x_specs = [c_specs[0], l_specs[0], c_specs[1], l_specs[1], c_specs[2], l_specs[2]]
        x_args = [xs[0], xs[1]] * 3
    else:
        x_specs = list(tile_specs(0, N_TILES))
        x_args = [xs[0]] * 3
    return pl.pallas_call(
        functools.partial(_pre_kernel, split),
        grid=(N_TILES,),
        in_specs=x_specs + [
            pl.BlockSpec((None, 1, 1, N_MOD * D_MODEL), lambda i: (layer, _mod_row(i), 0, 0)),
            pl.BlockSpec((None, 1, D_MODEL), lambda i: (layer, 0, 0), **const),
            pl.BlockSpec((None, D_MODEL, N_FEAT), lambda i: (layer, 0, 0), **const),
            pl.BlockSpec((None, D_MODEL, 128), lambda i: (layer, 0, 0), **const),
            pl.BlockSpec((None, 3, 3 * D_A), lambda i: (layer, 0, 0), **const),
            pl.BlockSpec((None, 1, 128), lambda i: (layer, 0, 0), **const),
            pl.BlockSpec((None, 1, 128), lambda i: (layer, 0, 0), **const),
        ],
        out_specs=[
            pl.BlockSpec((TM, 3 * D_A), lambda i: (i, 0)),
            pl.BlockSpec((TM, 3 * D_A), lambda i: (i, 0)),
            pl.BlockSpec((TM, 128), lambda i: (i, 0)),
        ],
        out_shape=[
            jax.ShapeDtypeStruct((N_TOK, 3 * D_A), BF16),
            jax.ShapeDtypeStruct((N_TOK, 3 * D_A), BF16),
            jax.ShapeDtypeStruct((N_TOK, 128), F32),
        ],
        compiler_params=pltpu.CompilerParams(
            dimension_semantics=("arbitrary",), vmem_limit_bytes=VMEM_LIMIT_BYTES),
        name=f"pre_l{layer}",
    )(*x_args, mod, norm_mix, w_main, w_ab, conv_qkv, a_row, dt_row)


def _delta_tile_row(rev, t):
    if not rev:
        return t
    lat = t - CTX_DTILES
    b = lat // LAT_DTILES_PER_SEQ
    j = LAT_DTILES_PER_SEQ - 1 - lat % LAT_DTILES_PER_SEQ
    return jnp.where(t < CTX_DTILES, t, CTX_DTILES + b * LAT_DTILES_PER_SEQ + j)


def _mm_inv(a, b):
    return _dot(a.astype(BF16), b.astype(BF16))


def _row_groups(x, size):
    return [x[g * size:(g + 1) * size] for g in range(x.shape[0] // size)]


def _delta_direction(rev, is_ctx, q_ref, k_ref, v_ref, gb_ref, o_ref, state0):
    row = lax.broadcasted_iota(jnp.int32, (CHUNK, CHUNK), 0)
    col = lax.broadcasted_iota(jnp.int32, (CHUNK, CHUNK), 1)
    incl = row <= col if rev else row >= col
    strict = row < col if rev else row > col
    eye = (row == col).astype(F32)
    base_mask = (row // INV_BASE) == (col // INV_BASE)
    levels = []
    blk = INV_BASE
    while blk < CHUNK:
        levels.append((blk, jnp.logical_and((row // (2 * blk)) == (col // (2 * blk)),
                                            (row // blk) != (col // blk))))
        blk *= 2
    rix = lax.broadcasted_iota(jnp.int32, (CHUNK, 128), 0)
    lane_beta = H_A if rev else 0
    lane_g = 3 * H_A if rev else 2 * H_A
    last = 0 if rev else CHUNK - 1

    chunk_order = range(CHUNKS_PER_TD - 1, -1, -1) if rev else range(CHUNKS_PER_TD)
    chunk_rows = [slice(cc * CHUNK, (cc + 1) * CHUNK) for cc in chunk_order]
    probs = [[dict(hd=hd, hcols=slice(hd * DK, (hd + 1) * DK)) for hd in range(H_A)]
             for _ in range(CHUNKS_PER_TD)]
    carry = dict(state=list(state0), mid=None)

    def prep(c):
        rows = chunk_rows[c]
        gb = gb_ref[rows, :]
        fwd_sum = gb
        s = 1
        while s < CHUNK:
            fwd_sum = fwd_sum + jnp.where(rix >= s, pltpu.roll(fwd_sum, s, 0), 0.0)
            s *= 2
        cum = fwd_sum[CHUNK - 1:CHUNK, :] - fwd_sum + gb if rev else fwd_sum
        cum_t = cum.T
        for p in probs[c]:
            hd = p["hd"]
            g_b = jnp.broadcast_to(cum[:, lane_g + hd:lane_g + hd + 1], (CHUNK, CHUNK))
            p.update(beta_b=jnp.broadcast_to(gb[:, lane_beta + hd:lane_beta + hd + 1], (CHUNK, CHUNK)),
                     g_b=g_b,
                     g_row=cum_t[lane_g + hd:lane_g + hd + 1, :],
                     g_tot=g_b[last:last + 1, :])

    def kq_stage(c):
        rows = chunk_rows[c]
        for p in probs[c]:
            k_bf = k_ref[rows, p["hcols"]]
            kb = (k_bf.astype(F32) * p["beta_b"]).astype(BF16)
            kq = _dot_nt(jnp.concatenate([kb, q_ref[rows, p["hcols"]]], axis=0), k_bf)
            decay = jnp.exp(jnp.minimum(p["g_b"] - p["g_row"], 0.0))
            p["a"] = jnp.where(strict, kq[0:CHUNK] * decay, 0.0)
            p["qk"] = jnp.where(incl, kq[CHUNK:2 * CHUNK] * decay, 0.0).astype(BF16)

    def base_first(c):
        for p in probs[c]:
            a0 = jnp.where(base_mask, p["a"], 0.0)
            p["inv"] = eye - a0
            p["pw"] = _mm_inv(a0, a0)

    def base_mid(c):
        for p in probs[c]:
            st = _mm_inv(jnp.concatenate([p["pw"], p["inv"]], axis=0), p["pw"])
            p["inv"] = p["inv"] + st[CHUNK:2 * CHUNK]
            p["pw"] = st[0:CHUNK]

    def base_last(c):
        for p in probs[c]:
            p["inv"] = p["inv"] + _mm_inv(p["inv"], p["pw"])

    upd_parity = 0 if rev else 1

    def merge_first(size, off_mask, c):
        for p in probs[c]:
            p["groups"] = _row_groups(p["inv"], size)
            lhs = jnp.concatenate(p["groups"][upd_parity::2], axis=0)
            p["tmp"] = _mm_inv(lhs, jnp.where(off_mask, p["a"], 0.0))

    def merge_second(size, c):
        for p in probs[c]:
            upd = _row_groups(_mm_inv(p["tmp"], p["inv"]), size)
            new = [g - upd[i // 2] if i % 2 == upd_parity else g for i, g in enumerate(p["groups"])]
            p["inv"] = jnp.concatenate(new, axis=0)

    def to_bf16(c):
        for p in probs[c]:
            p["inv"] = p["inv"].astype(BF16)
            p["a"] = p["a"].astype(BF16)

    def merge_first_bf16(size, mask_bf, c):
        for p in probs[c]:
            p["groups"] = _row_groups(p["inv"], size)
            lhs = jnp.concatenate(p["groups"][upd_parity::2], axis=0)
            p["tmp"] = _dot(lhs, p["a"] * mask_bf).astype(BF16)

    def merge_second_bf16(size, c):
        for p in probs[c]:
            upd = _row_groups(_dot(p["tmp"], p["inv"]), size)
            new = [(g.astype(F32) - upd[i // 2]).astype(BF16) if i % 2 == upd_parity else g
                   for i, g in enumerate(p["groups"])]
            p["inv"] = jnp.concatenate(new, axis=0)

    def solve(c):
        rows = chunk_rows[c]
        for p in probs[c]:
            beta_b = p["beta_b"]
            rhs = jnp.concatenate(
                [v_ref[rows, p["hcols"]].astype(F32) * beta_b,
                 k_ref[rows, p["hcols"]].astype(F32) * beta_b * jnp.exp(p["g_b"])], axis=-1)
            sol = _dot(p["inv"].astype(BF16), rhs.astype(BF16))
            p["u"] = sol[:, 0:DV]
            p["w"] = sol[:, DV:2 * DV].astype(BF16)

    def rec_first(c):
        if c == CHUNKS_PER_TD // 2:
            carry["mid"] = carry["state"]
            carry["state"] = [jnp.where(is_ctx, 0.0, st) for st in carry["state"]]
        for p in probs[c]:
            p["st_bf"] = carry["state"][p["hd"]].astype(BF16)
            p["ws"] = _dot(p["w"], p["st_bf"])

    def rec_second(c):
        state = list(carry["state"])
        rows = chunk_rows[c]
        for p in probs[c]:
            g_b, g_tot = p["g_b"], p["g_tot"]
            q_dec = (q_ref[rows, p["hcols"]].astype(F32) * jnp.exp(g_b)).astype(BF16)
            k_dec = (k_ref[rows, p["hcols"]].astype(F32) * jnp.exp(g_tot - g_b)).astype(BF16)
            v_new = (p["u"] - p["ws"]).astype(BF16)
            o_ref[rows, p["hcols"]] = _dot(jnp.concatenate([q_dec, p["qk"]], axis=1),
                                           jnp.concatenate([p["st_bf"], v_new], axis=0))
            state[p["hd"]] = state[p["hd"]] * jnp.exp(g_tot) + _dot_tn(k_dec, v_new)
        carry["state"] = state

    stages = [prep, kq_stage, base_first]
    n = 2
    while 2 * n < INV_BASE:
        stages.append(base_mid)
        n *= 2
    stages.append(base_last)
    in_bf16 = False
    for size, off_mask in levels:
        if size % BF16_TILE_ROWS == 0 and not in_bf16:
            stages.append(to_bf16)
            in_bf16 = True
        if in_bf16:
            mask_bf = jnp.where(off_mask, 1.0, 0.0).astype(BF16)
            stages.append(functools.partial(merge_first_bf16, size, mask_bf))
            stages.append(functools.partial(merge_second_bf16, size))
        else:
            stages.append(functools.partial(merge_first, size, off_mask))
            stages.append(functools.partial(merge_second, size))
    stages.append(solve)
    return stages, rec_first, rec_second, carry


def _delta_kernel(qf_ref, kf_ref, vf_ref, gbf_ref, qb_ref, kb_ref, vb_ref, gbb_ref, s0_ref,
                  ns_in_ref, of_ref, ob_ref, ns_ref, s_scr):
    del ns_in_ref
    t = pl.program_id(0)
    is_ctx = t < CTX_DTILES

    @pl.when(is_ctx)
    def _():
        s_scr[...] = jnp.zeros_like(s_scr)

    @pl.when(jnp.logical_and(t >= CTX_DTILES, (t - CTX_DTILES) % LAT_DTILES_PER_SEQ == 0))
    def _():
        s_scr[...] = s0_ref[...]

    dirs = [
        _delta_direction(False, is_ctx, qf_ref, kf_ref, vf_ref, gbf_ref, of_ref,
                         [s_scr[0, hd] for hd in range(H_A)]),
        _delta_direction(True, is_ctx, qb_ref, kb_ref, vb_ref, gbb_ref, ob_ref,
                         [s_scr[1, hd] for hd in range(H_A)]),
    ]
    for k in range(len(dirs[0][0])):
        for stages, _, _, _ in dirs:
            for c in range(CHUNKS_PER_TD):
                stages[k](c)
    for c in range(CHUNKS_PER_TD):
        for _, rec_first, _, _ in dirs:
            rec_first(c)
        for _, _, rec_second, _ in dirs:
            rec_second(c)

    for d, (_, _, _, carry) in enumerate(dirs):
        for hd in range(H_A):
            s_scr[d, hd] = carry["state"][hd]

    @pl.when(is_ctx)
    def _():
        for d, (_, _, _, carry) in enumerate(dirs):
            first, second = (1, 0) if d == 1 else (0, 1)
            for hd in range(H_A):
                ns_ref[first, d, hd] = carry["mid"][hd]
                ns_ref[second, d, hd] = carry["state"][hd]


def _delta_call(layer, qkv, gb, state_delta, new_state):
    def tile_specs(rev):
        def spec(width, col_block):
            return pl.BlockSpec((TD, width), lambda t: (_delta_tile_row(rev, t), col_block))
        return [spec(D_A, 0), spec(D_A, 1), spec(D_A, 2), spec(128, 0)]

    def s0_index(t):
        b = jnp.maximum(t - CTX_DTILES, 0) // LAT_DTILES_PER_SEQ
        return (b, layer, 0, 0, 0, 0)

    def ns_index(t):
        return (jnp.minimum(t, CTX_DTILES - 1), layer, 0, 0, 0, 0)

    return pl.pallas_call(
        _delta_kernel,
        grid=(N_DTILES,),
        in_specs=tile_specs(False) + tile_specs(True) + [
            pl.BlockSpec((None, None, 2, H_A, DK, DV), s0_index),
            pl.BlockSpec(memory_space=pl.ANY),
        ],
        out_specs=[
            pl.BlockSpec((TD, D_A), lambda t: (_delta_tile_row(False, t), 0)),
            pl.BlockSpec((TD, D_A), lambda t: (_delta_tile_row(True, t), 0)),
            pl.BlockSpec((SEQ_PER_CTX_DTILE, None, 2, H_A, DK, DV), ns_index),
        ],
        out_shape=[
            jax.ShapeDtypeStruct((N_TOK, D_A), F32),
            jax.ShapeDtypeStruct((N_TOK, D_A), F32),
            jax.ShapeDtypeStruct(new_state.shape, F32),
        ],
        scratch_shapes=[pltpu.VMEM((2, H_A, DK, DV), F32)],
        input_output_aliases={9: 2},
        compiler_params=pltpu.CompilerParams(
            dimension_semantics=("arbitrary",), vmem_limit_bytes=VMEM_LIMIT_BYTES),
        name=f"delta_l{layer}",
    )(qkv, qkv, qkv, gb, qkv, qkv, qkv, gb, state_delta, new_state)


def _post_kernel(final, split, *refs):
    is_ctx = pl.program_id(0) < CTX_TILES
    (of_ref, ob_ref, rest_ref, mod_ref, onorm_ref, ws_ref, bs_ref, wout_ref, nffn_ref, wff1_ref,
     wff2_ref, nfin_ref) = refs[1 + split:13 + split]
    out_refs = refs[13 + split:]
    m = mod_ref[0]
    ga1 = m[:, 2 * D_MODEL:3 * D_MODEL]
    sh2 = m[:, 3 * D_MODEL:4 * D_MODEL]
    coef2 = nffn_ref[...] * (1.0 + m[:, 4 * D_MODEL:5 * D_MODEL])
    ga2 = m[:, 5 * D_MODEL:6 * D_MODEL]
    onw = onorm_ref[...]
    bs = bs_ref[...]

    def head(rows):
        if split:
            x = jnp.where(is_ctx, refs[0][rows, :], refs[1][rows, :])
        else:
            x = refs[0][rows, :]
        o = of_ref[rows, :] + ob_ref[rows, :]
        mix_in = []
        for hd in range(H_A):
            oh = o[:, hd * DV:(hd + 1) * DV]
            oh = oh * (lax.rsqrt(jnp.mean(oh * oh, axis=-1, keepdims=True) + EPS) * onw)
            gate = _silu(rest_ref[rows, hd * DV:(hd + 1) * DV].astype(F32))
            mix_in.append((oh * gate).astype(BF16))
        for g in range(G_B):
            ug = _gelu_tanh(rest_ref[rows, D_A + g * C_B:D_A + (g + 1) * C_B].astype(F32))
            vg = _gelu_tanh(rest_ref[rows, 2 * D_A + g * C_B:2 * D_A + (g + 1) * C_B].astype(F32))
            dv = vg - jnp.mean(vg, axis=-1, keepdims=True)
            var = jnp.mean(dv * dv, axis=-1, keepdims=True)
            vgn = (dv * lax.rsqrt(var + EPS)).astype(BF16)
            parts = []
            for c in range(POST_SUB // CHUNK_MLP):
                crows = slice(c * CHUNK_MLP, (c + 1) * CHUNK_MLP)
                mixed = _dot(ws_ref[g], vgn[crows]) + bs[:, g:g + 1]
                parts.append((ug[crows] * mixed).astype(BF16))
            mix_in.append(jnp.concatenate(parts, axis=0))
        mix = jnp.concatenate(mix_in, axis=-1)
        x1 = x + ga1 * _dot(mix, wout_ref[...])
        ms = jnp.mean(x1 * x1, axis=-1, keepdims=True)
        return x1, ((x1 * lax.rsqrt(ms + EPS)) * coef2 + sh2).astype(BF16)

    subs = [slice(r0, r0 + POST_SUB) for r0 in range(0, TM, POST_SUB)]
    ys = []
    nxt = head(subs[0])
    for s, rows in enumerate(subs):
        x1, h2 = nxt
        a = jnp.maximum(_dot(h2, wff1_ref[...]), 0.0)
        if s + 1 < len(subs):
            nxt = head(subs[s + 1])
        x2 = x1 + ga2 * _dot((a * a).astype(BF16), wff2_ref[...])
        if final:
            ms2 = jnp.mean(x2 * x2, axis=-1, keepdims=True)
            x2 = (x2 * lax.rsqrt(ms2 + EPS)) * nfin_ref[...]
        ys.append((rows, x2))

    if not final:
        for rows, y in ys:
            out_refs[0][rows, :] = y
        return
    yc_ref, yl_ref = out_refs

    @pl.when(is_ctx)
    def _():
        for rows, y in ys:
            yc_ref[rows, :] = y

    @pl.when(jnp.logical_not(is_ctx))
    def _():
        for rows, y in ys:
            yl_ref[rows, :] = y


def _post_call(layer, final, xs, o_fwd, o_bwd, rest, mod, o_norm, ws, bs_t, w_out, norm_ffn, w_ff1,
               w_ff2, norm_final):
    split = len(xs) == 2
    const = dict(pipeline_mode=pl.Buffered(1))
    if split:
        x_specs = [pl.BlockSpec((TM, D_MODEL), lambda i: (jnp.minimum(i, CTX_TILES - 1), 0)),
                   pl.BlockSpec((TM, D_MODEL), lambda i: (jnp.maximum(i - CTX_TILES, 0), 0))]
    else:
        x_specs = [pl.BlockSpec((TM, D_MODEL), lambda i: (i, 0))]
    if final:
        out_specs = [
            pl.BlockSpec((TM, D_MODEL), lambda i: (jnp.minimum(i, CTX_TILES - 1), 0)),
            pl.BlockSpec((TM, D_MODEL), lambda i: (jnp.maximum(i - CTX_TILES, 0), 0)),
        ]
        out_shape = [jax.ShapeDtypeStruct((N_CTX, D_MODEL), F32),
                     jax.ShapeDtypeStruct((N_LAT, D_MODEL), F32)]
    else:
        out_specs = pl.BlockSpec((TM, D_MODEL), lambda i: (i, 0))
        out_shape = jax.ShapeDtypeStruct((N_TOK, D_MODEL), F32)
    return pl.pallas_call(
        functools.partial(_post_kernel, final, split),
        grid=(N_TILES,),
        in_specs=x_specs + [
            pl.BlockSpec((TM, D_A), lambda i: (i, 0)),
            pl.BlockSpec((TM, D_A), lambda i: (i, 0)),
            pl.BlockSpec((TM, 3 * D_A), lambda i: (i, 0)),
            pl.BlockSpec((None, 1, 1, N_MOD * D_MODEL), lambda i: (layer, _mod_row(i), 0, 0)),
            pl.BlockSpec((None, 1, DV), lambda i: (layer, 0, 0), **const),
            pl.BlockSpec((None, G_B, CHUNK_MLP, CHUNK_MLP), lambda i: (layer, 0, 0, 0), **const),
            pl.BlockSpec((None, CHUNK_MLP, 128), lambda i: (layer, 0, 0), **const),
            pl.BlockSpec((None, D_MODEL, D_MODEL), lambda i: (layer, 0, 0), **const),
            pl.BlockSpec((None, 1, D_MODEL), lambda i: (layer, 0, 0), **const),
            pl.BlockSpec((None, D_MODEL, D_FF), lambda i: (layer, 0, 0), **const),
            pl.BlockSpec((None, D_FF, D_MODEL), lambda i: (layer, 0, 0), **const),
            pl.BlockSpec((1, D_MODEL), lambda i: (0, 0), **const),
        ],
        out_specs=out_specs,
        out_shape=out_shape,
        compiler_params=pltpu.CompilerParams(
            dimension_semantics=("arbitrary",), vmem_limit_bytes=VMEM_LIMIT_BYTES),
        name=f"post_l{layer}",
    )(*xs, o_fwd, o_bwd, rest, mod, o_norm, ws, bs_t, w_out, norm_ffn, w_ff1, w_ff2, norm_final)


def kernel(x_prompt, x_sample, state_delta, c, c_ctx, w_mod, b_mod, norm_mix, w_in, conv_qkv, a_log,
           dt_bias, o_norm, w_spatial, b_spatial, w_out, norm_ffn, w_ff1, w_ff2, norm_final):
    cond8 = jnp.concatenate([c_ctx[None, :], c, jnp.zeros((8 - 1 - LAT_B, D_MODEL), F32)], axis=0)

    n_qkvg = 4 * D_A
    w_main = jnp.concatenate([w_in[:, :, :n_qkvg], w_in[:, :, n_qkvg + 4 * H_A:]], axis=-1).astype(BF16)
    w_ab = jnp.pad(w_in[:, :, n_qkvg:n_qkvg + 4 * H_A], ((0, 0), (0, 0), (0, 128 - 4 * H_A))).astype(BF16)
    a_row = jnp.pad(a_log.reshape(DEPTH, 1, 2 * H_A), ((0, 0), (0, 0), (2 * H_A, 128 - 4 * H_A)))
    dt_row = jnp.pad(dt_bias.reshape(DEPTH, 1, 2 * H_A), ((0, 0), (0, 0), (2 * H_A, 128 - 4 * H_A)))
    bs_t = jnp.pad(jnp.swapaxes(b_spatial, 1, 2), ((0, 0), (0, 0), (0, 128 - G_B)))
    ws_bf = w_spatial.astype(BF16)
    w_out_bf = w_out.astype(BF16)
    w_ff1_bf = w_ff1.astype(BF16)
    w_ff2_bf = w_ff2.astype(BF16)
    norm_mix3 = norm_mix.reshape(DEPTH, 1, D_MODEL)
    norm_ffn3 = norm_ffn.reshape(DEPTH, 1, D_MODEL)
    o_norm3 = o_norm.reshape(DEPTH, 1, DV)
    norm_final2 = norm_final.reshape(1, D_MODEL)

    mod = _modulation(cond8, w_mod, b_mod).reshape(DEPTH, 8, 1, N_MOD * D_MODEL)

    xs = (x_prompt.reshape(N_CTX, D_MODEL), x_sample.reshape(N_LAT, D_MODEL))
    new_state = jnp.zeros((CTX_B, DEPTH, 2, H_A, DK, DV), F32)
    for layer in range(DEPTH):
        qkv, rest, gb = _pre_call(layer, xs, mod, norm_mix3, w_main, w_ab, conv_qkv, a_row, dt_row)
        o_fwd, o_bwd, new_state = _delta_call(layer, qkv, gb, state_delta, new_state)
        out = _post_call(layer, layer == DEPTH - 1, xs, o_fwd, o_bwd, rest, mod, o_norm3, ws_bf, bs_t,
                         w_out_bf, norm_ffn3, w_ff1_bf, w_ff2_bf, norm_final2)
        xs = (out,)

    y_prompt, y_sample = out
    return (y_prompt.reshape(CTX_B, CTX_T, D_MODEL), y_sample.reshape(LAT_B, LAT_T, D_MODEL), new_state)
```

```python
import functools

import jax
import jax.numpy as jnp
from jax import lax
from jax.experimental import pallas as pl
from jax.experimental.pallas import tpu as pltpu

F32 = jnp.float32
BF16 = jnp.bfloat16

D_MODEL = 1024
DEPTH = 4
N_MOD = 6
H_A = 4
DK = 128
DV = 128
D_A = H_A * DK
G_B = 4
C_B = 128
D_B = G_B * C_B
CHUNK_MLP = 128
D_FF = 4 * D_MODEL
EPS = 1e-6

CTX_B, CTX_T = 32, 256
LAT_B, LAT_T = 2, 4096
N_CTX = CTX_B * CTX_T
N_LAT = LAT_B * LAT_T
N_TOK = N_CTX + N_LAT

LANES = 128
MOD_TN = 3072
TM = 512
N_TILES = N_TOK // TM
CTX_TILES = N_CTX // TM
LAT_TILES_PER_SEQ = LAT_T // TM
HALO = 8
POST_SUB = 256

CHUNK = 128
INV_BASE = 8
BF16_TILE_ROWS = 16
TD = 512
CHUNKS_PER_TD = TD // CHUNK
SEQ_PER_CTX_DTILE = TD // CTX_T
CTX_DTILES = N_CTX // TD
LAT_DTILES_PER_SEQ = LAT_T // TD
N_DTILES = N_TOK // TD

N_FEAT = 6 * D_A
VMEM_LIMIT_BYTES = 60000 * 1024


def _sigmoid(x):
    return 1.0 / (1.0 + jnp.exp(-x))


def _silu(x):
    hx = 0.5 * x
    return hx + hx * jnp.tanh(hx)


_GELU_C0 = 0.7978845608028654
_GELU_C1 = _GELU_C0 * 0.044715


def _gelu_tanh(x):
    hx = 0.5 * x
    return hx + hx * jnp.tanh(x * (_GELU_C0 + _GELU_C1 * (x * x)))


def _softplus(x):
    return jnp.maximum(x, 0.0) + jnp.log1p(jnp.exp(-jnp.abs(x)))


def _dot(a, b):
    return jnp.dot(a, b, preferred_element_type=F32)


def _dot_nt(a, b):
    return lax.dot_general(a, b, (((1,), (1,)), ((), ())), preferred_element_type=F32)


def _dot_tn(a, b):
    return lax.dot_general(a, b, (((0,), (0,)), ((), ())), preferred_element_type=F32)


def _mod_kernel(cond_ref, w_ref, b_ref, o_ref):
    c = cond_ref[...]
    s = (c * _sigmoid(c)).astype(BF16)
    o_ref[0] = _dot(s, w_ref[0].astype(BF16)) + b_ref[0]


def _modulation(cond8, w_mod, b_mod):
    tn = MOD_TN
    return pl.pallas_call(
        _mod_kernel,
        grid=(DEPTH, (N_MOD * D_MODEL) // tn),
        in_specs=[
            pl.BlockSpec((8, D_MODEL), lambda l, j: (0, 0)),
            pl.BlockSpec((1, D_MODEL, tn), lambda l, j: (l, 0, j)),
            pl.BlockSpec((1, 1, tn), lambda l, j: (l, 0, j)),
        ],
        out_specs=pl.BlockSpec((1, 8, tn), lambda l, j: (l, 0, j)),
        out_shape=jax.ShapeDtypeStruct((DEPTH, 8, N_MOD * D_MODEL), F32),
        compiler_params=pltpu.CompilerParams(
            dimension_semantics=("arbitrary", "arbitrary"), vmem_limit_bytes=VMEM_LIMIT_BYTES),
        name="modulation",
    )(cond8, w_mod, b_mod.reshape(DEPTH, 1, N_MOD * D_MODEL))


def _mod_row(i):
    return jnp.where(i < CTX_TILES, 0, 1 + (i - CTX_TILES) // LAT_TILES_PER_SEQ)


def _pre_kernel(split, *refs):
    i = pl.program_id(0)
    is_ctx = i < CTX_TILES
    if split:
        (xc_ref, xl_ref, xcp_ref, xlp_ref, xcn_ref, xln_ref, mod_ref, nw_ref, w_ref, wab_ref,
         conv_ref, arow_ref, dtrow_ref, qkv_ref, rest_ref, gb_ref) = refs
        x = jnp.where(is_ctx, xc_ref[...], xl_ref[...])
        x_prev = jnp.where(is_ctx, xcp_ref[...], xlp_ref[...])
        x_next = jnp.where(is_ctx, xcn_ref[...], xln_ref[...])
    else:
        (x_ref, xp_ref, xn_ref, mod_ref, nw_ref, w_ref, wab_ref,
         conv_ref, arow_ref, dtrow_ref, qkv_ref, rest_ref, gb_ref) = refs
        x, x_prev, x_next = x_ref[...], xp_ref[...], xn_ref[...]
    m = mod_ref[0]
    sh1 = m[:, 0:D_MODEL]
    coef = nw_ref[...] * (1.0 + m[:, D_MODEL:2 * D_MODEL])

    def norm_mod(xx):
        ms = jnp.mean(xx * xx, axis=-1, keepdims=True)
        return ((xx * lax.rsqrt(ms + EPS)) * coef + sh1).astype(BF16)

    h = norm_mod(x)
    hh = norm_mod(jnp.concatenate([x_prev, x_next], axis=0))
    h_ext = jnp.concatenate([h, hh], axis=0)

    lat_j = (i - CTX_TILES) % LAT_TILES_PER_SEQ
    starts_seq = jnp.logical_or(is_ctx, lat_j == 0)
    ends_seq = jnp.logical_or(is_ctx, lat_j == LAT_TILES_PER_SEQ - 1)
    r8 = lax.broadcasted_iota(jnp.int32, (HALO, 1), 0)

    def shifted(p, prev_row, next_row):
        dn = pltpu.roll(p, 1, 0)
        up = pltpu.roll(p, TM - 1, 0)
        dn_parts, up_parts = [], []
        for b in range(0, TM, CTX_T):
            slab = dn[b:b + HALO]
            if b == 0:
                slab = jnp.where(r8 == 0, jnp.where(starts_seq, 0.0, prev_row), slab)
            else:
                slab = jnp.where(jnp.logical_and(r8 == 0, is_ctx), 0.0, slab)
            dn_parts += [slab, dn[b + HALO:b + CTX_T]]
            e = b + CTX_T
            slab = up[e - HALO:e]
            if e == TM:
                slab = jnp.where(r8 == HALO - 1, jnp.where(ends_seq, 0.0, next_row), slab)
            else:
                slab = jnp.where(jnp.logical_and(r8 == HALO - 1, is_ctx), 0.0, slab)
            up_parts += [up[b:e - HALO], slab]
        return jnp.concatenate(dn_parts, axis=0), jnp.concatenate(up_parts, axis=0)

    cw = conv_ref[...]
    for blk in range(3):
        cols = slice(blk * D_A, (blk + 1) * D_A)
        pe = _dot(h_ext, w_ref[:, cols])
        rcols = slice((3 + blk) * D_A, (4 + blk) * D_A)
        rest_ref[:, blk * D_A:(blk + 1) * D_A] = _dot(h, w_ref[:, rcols]).astype(BF16)
        if blk == 0:
            ab = _dot(h, wab_ref[...])
            lane = lax.broadcasted_iota(jnp.int32, (TM, LANES), 1)
            g_log = -jnp.exp(arow_ref[...]) * _softplus(ab + dtrow_ref[...])
            gb_ref[...] = jnp.where(lane < 2 * H_A, _sigmoid(ab), g_log)
        p = pe[0:TM]
        xm1, xp1 = shifted(p, pe[TM + HALO - 1:TM + HALO], pe[TM + HALO:TM + HALO + 1])
        y = _silu(cw[0:1, cols] * xm1 + cw[1:2, cols] * p + cw[2:3, cols] * xp1)
        if blk < 2:
            scale = DK ** -0.5 if blk == 0 else 1.0
            for hd in range(H_A):
                yh = y[:, hd * DK:(hd + 1) * DK]
                inv_norm = lax.rsqrt(jnp.sum(yh * yh, axis=-1, keepdims=True) + EPS) * scale
                qkv_ref[:, blk * D_A + hd * DK:blk * D_A + (hd + 1) * DK] = (yh * inv_norm).astype(BF16)
        else:
            qkv_ref[:, cols] = y.astype(BF16)


def _pre_call(layer, xs, mod, norm_mix, w_main, w_ab, conv_qkv, a_row, dt_row):
    split = len(xs) == 2
    bpt = TM // HALO
    const = dict(pipeline_mode=pl.Buffered(1))

    def tile_specs(first_tile, n_tiles):
        def local(i):
            return jnp.clip(i - first_tile, 0, n_tiles - 1)
        return (
            pl.BlockSpec((TM, D_MODEL), lambda i: (local(i), 0)),
            pl.BlockSpec((HALO, D_MODEL), lambda i: (jnp.maximum(local(i) * bpt - 1, 0), 0)),
            pl.BlockSpec((HALO, D_MODEL),
                         lambda i: (jnp.minimum((local(i) + 1) * bpt, n_tiles * bpt - 1), 0)),
        )

    if split:
        c_specs = tile_specs(0, CTX_TILES)
        l_specs = tile_specs(CTX_TILES, N_TILES - CTX_TILES)
        x_specs = [c_specs[0], l_specs[0], c_specs[1], l_specs[1], c_specs[2], l_specs[2]]
        x_args = [xs[0], xs[1]] * 3
    else:
        x_specs = list(tile_specs(0, N_TILES))
        x_args = [xs[0]] * 3
    return pl.pallas_call(
        functools.partial(_pre_kernel, split),
        grid=(N_TILES,),
        in_specs=x_specs + [
            pl.BlockSpec((None, 1, 1, N_MOD * D_MODEL), lambda i: (layer, _mod_row(i), 0, 0)),
            pl.BlockSpec((None, 1, D_MODEL), lambda i: (layer, 0, 0), **const),
            pl.BlockSpec((None, D_MODEL, N_FEAT), lambda i: (layer, 0, 0), **const),
            pl.BlockSpec((None, D_MODEL, LANES), lambda i: (layer, 0, 0), **const),
            pl.BlockSpec((None, 3, 3 * D_A), lambda i: (layer, 0, 0), **const),
            pl.BlockSpec((None, 1, LANES), lambda i: (layer, 0, 0), **const),
            pl.BlockSpec((None, 1, LANES), lambda i: (layer, 0, 0), **const),
        ],
        out_specs=[
            pl.BlockSpec((TM, 3 * D_A), lambda i: (i, 0)),
            pl.BlockSpec((TM, 3 * D_A), lambda i: (i, 0)),
            pl.BlockSpec((TM, LANES), lambda i: (i, 0)),
        ],
        out_shape=[
            jax.ShapeDtypeStruct((N_TOK, 3 * D_A), BF16),
            jax.ShapeDtypeStruct((N_TOK, 3 * D_A), BF16),
            jax.ShapeDtypeStruct((N_TOK, LANES), F32),
        ],
        compiler_params=pltpu.CompilerParams(
            dimension_semantics=("arbitrary",), vmem_limit_bytes=VMEM_LIMIT_BYTES),
        name=f"pre_l{layer}",
    )(*x_args, mod, norm_mix, w_main, w_ab, conv_qkv, a_row, dt_row)


def _delta_tile_row(rev, t):
    if not rev:
        return t
    lat = t - CTX_DTILES
    b = lat // LAT_DTILES_PER_SEQ
    j = LAT_DTILES_PER_SEQ - 1 - lat % LAT_DTILES_PER_SEQ
    return jnp.where(t < CTX_DTILES, t, CTX_DTILES + b * LAT_DTILES_PER_SEQ + j)


def _mm_inv(a, b):
    return _dot(a.astype(BF16), b.astype(BF16))


def _row_groups(x, size):
    return [x[g * size:(g + 1) * size] for g in range(x.shape[0] // size)]


def _delta_direction(rev, is_ctx, q_ref, k_ref, v_ref, gb_ref, o_ref, state0):
    row = lax.broadcasted_iota(jnp.int32, (CHUNK, CHUNK), 0)
    col = lax.broadcasted_iota(jnp.int32, (CHUNK, CHUNK), 1)
    incl = row <= col if rev else row >= col
    strict = row < col if rev else row > col
    eye = (row == col).astype(F32)
    base_mask = (row // INV_BASE) == (col // INV_BASE)
    levels = []
    blk = INV_BASE
    while blk < CHUNK:
        levels.append((blk, jnp.logical_and((row // (2 * blk)) == (col // (2 * blk)),
                                            (row // blk) != (col // blk))))
        blk *= 2
    rix = lax.broadcasted_iota(jnp.int32, (CHUNK, LANES), 0)
    lane_beta = H_A if rev else 0
    lane_g = 3 * H_A if rev else 2 * H_A
    last = 0 if rev else CHUNK - 1

    chunk_order = range(CHUNKS_PER_TD - 1, -1, -1) if rev else range(CHUNKS_PER_TD)
    chunk_rows = [slice(cc * CHUNK, (cc + 1) * CHUNK) for cc in chunk_order]
    probs = [[dict(hd=hd, hcols=slice(hd * DK, (hd + 1) * DK)) for hd in range(H_A)]
             for _ in range(CHUNKS_PER_TD)]
    carry = dict(state=list(state0), mid=None)

    def prep(c):
        rows = chunk_rows[c]
        gb = gb_ref[rows, :]
        fwd_sum = gb
        s = 1
        while s < CHUNK:
            fwd_sum = fwd_sum + jnp.where(rix >= s, pltpu.roll(fwd_sum, s, 0), 0.0)
            s *= 2
        cum = fwd_sum[CHUNK - 1:CHUNK, :] - fwd_sum + gb if rev else fwd_sum
        cum_t = cum.T
        for p in probs[c]:
            hd = p["hd"]
            g_b = jnp.broadcast_to(cum[:, lane_g + hd:lane_g + hd + 1], (CHUNK, CHUNK))
            p.update(beta_b=jnp.broadcast_to(gb[:, lane_beta + hd:lane_beta + hd + 1], (CHUNK, CHUNK)),
                     g_b=g_b,
                     g_row=cum_t[lane_g + hd:lane_g + hd + 1, :],
                     g_tot=g_b[last:last + 1, :])

    def kq_stage(c):
        rows = chunk_rows[c]
        for p in probs[c]:
            k_bf = k_ref[rows, p["hcols"]]
            kb = (k_bf.astype(F32) * p["beta_b"]).astype(BF16)
            kq = _dot_nt(jnp.concatenate([kb, q_ref[rows, p["hcols"]]], axis=0), k_bf)
            decay = jnp.exp(jnp.minimum(p["g_b"] - p["g_row"], 0.0))
            p["a"] = jnp.where(strict, kq[0:CHUNK] * decay, 0.0)
            p["qk"] = jnp.where(incl, kq[CHUNK:2 * CHUNK] * decay, 0.0).astype(BF16)

    def base_first(c):
        for p in probs[c]:
            a0 = jnp.where(base_mask, p["a"], 0.0)
            p["inv"] = eye - a0
            p["pw"] = _mm_inv(a0, a0).astype(BF16)

    def base_mid(c):
        for p in probs[c]:
            st = _dot(jnp.concatenate([p["pw"], p["inv"].astype(BF16)], axis=0), p["pw"])
            p["inv"] = p["inv"] + st[CHUNK:2 * CHUNK]
            p["pw"] = st[0:CHUNK].astype(BF16)

    def base_last(c):
        for p in probs[c]:
            p["inv"] = p["inv"] + _dot(p["inv"].astype(BF16), p["pw"])

    upd_parity = 0 if rev else 1

    def merge_first(size, off_mask, c):
        for p in probs[c]:
            p["groups"] = _row_groups(p["inv"], size)
            lhs = jnp.concatenate(p["groups"][upd_parity::2], axis=0)
            p["tmp"] = _mm_inv(lhs, jnp.where(off_mask, p["a"], 0.0))

    def merge_second(size, c):
        for p in probs[c]:
            upd = _row_groups(_mm_inv(p["tmp"], p["inv"]), size)
            new = [g - upd[i // 2] if i % 2 == upd_parity else g for i, g in enumerate(p["groups"])]
            p["inv"] = jnp.concatenate(new, axis=0)

    def to_bf16(c):
        for p in probs[c]:
            p["inv"] = p["inv"].astype(BF16)
            p["a"] = p["a"].astype(BF16)

    def merge_first_bf16(size, mask_bf, c):
        for p in probs[c]:
            p["groups"] = _row_groups(p["inv"], size)
            lhs = jnp.concatenate(p["groups"][upd_parity::2], axis=0)
            p["tmp"] = _dot(lhs, p["a"] * mask_bf).astype(BF16)

    def merge_second_bf16(size, c):
        for p in probs[c]:
            upd = _row_groups(_dot(p["tmp"], p["inv"]), size)
            new = [(g.astype(F32) - upd[i // 2]).astype(BF16) if i % 2 == upd_parity else g
                   for i, g in enumerate(p["groups"])]
            p["inv"] = jnp.concatenate(new, axis=0)

    def solve(c):
        rows = chunk_rows[c]
        for p in probs[c]:
            beta_b = p["beta_b"]
            rhs = jnp.concatenate(
                [v_ref[rows, p["hcols"]].astype(F32) * beta_b,
                 k_ref[rows, p["hcols"]].astype(F32) * beta_b * jnp.exp(p["g_b"])], axis=-1)
            sol = _dot(p["inv"].astype(BF16), rhs.astype(BF16))
            p["u"] = sol[:, 0:DV]
            p["w"] = sol[:, DV:2 * DV].astype(BF16)

    def rec_first(c):
        if c == CHUNKS_PER_TD // 2:
            carry["mid"] = carry["state"]
            carry["state"] = [jnp.where(is_ctx, 0.0, st) for st in carry["state"]]
        for p in probs[c]:
            p["st_bf"] = carry["state"][p["hd"]].astype(BF16)
            p["ws"] = _dot(p["w"], p["st_bf"])

    def rec_second(c):
        state = list(carry["state"])
        rows = chunk_rows[c]
        for p in probs[c]:
            g_b, g_tot = p["g_b"], p["g_tot"]
            q_dec = (q_ref[rows, p["hcols"]].astype(F32) * jnp.exp(g_b)).astype(BF16)
            k_dec = (k_ref[rows, p["hcols"]].astype(F32) * jnp.exp(g_tot - g_b)).astype(BF16)
            v_new = (p["u"] - p["ws"]).astype(BF16)
            o_ref[rows, p["hcols"]] = _dot(jnp.concatenate([q_dec, p["qk"]], axis=1),
                                           jnp.concatenate([p["st_bf"], v_new], axis=0))
            state[p["hd"]] = state[p["hd"]] * jnp.exp(g_tot) + _dot_tn(k_dec, v_new)
        carry["state"] = state

    stages = [prep, kq_stage, base_first]
    n = 2
    while 2 * n < INV_BASE:
        stages.append(base_mid)
        n *= 2
    stages.append(base_last)
    in_bf16 = False
    for size, off_mask in levels:
        if size % BF16_TILE_ROWS == 0 and not in_bf16:
            stages.append(to_bf16)
            in_bf16 = True
        if in_bf16:
            mask_bf = jnp.where(off_mask, 1.0, 0.0).astype(BF16)
            stages.append(functools.partial(merge_first_bf16, size, mask_bf))
            stages.append(functools.partial(merge_second_bf16, size))
        else:
            stages.append(functools.partial(merge_first, size, off_mask))
            stages.append(functools.partial(merge_second, size))
    stages.append(solve)
    return stages, rec_first, rec_second, carry


def _delta_kernel(qf_ref, kf_ref, vf_ref, gbf_ref, qb_ref, kb_ref, vb_ref, gbb_ref, s0_ref,
                  ns_in_ref, of_ref, ob_ref, ns_ref, s_scr):
    del ns_in_ref
    t = pl.program_id(0)
    is_ctx = t < CTX_DTILES

    @pl.when(is_ctx)
    def _():
        s_scr[...] = jnp.zeros_like(s_scr)

    @pl.when(jnp.logical_and(t >= CTX_DTILES, (t - CTX_DTILES) % LAT_DTILES_PER_SEQ == 0))
    def _():
        s_scr[...] = s0_ref[...]

    dirs = [
        _delta_direction(False, is_ctx, qf_ref, kf_ref, vf_ref, gbf_ref, of_ref,
                         [s_scr[0, hd] for hd in range(H_A)]),
        _delta_direction(True, is_ctx, qb_ref, kb_ref, vb_ref, gbb_ref, ob_ref,
                         [s_scr[1, hd] for hd in range(H_A)]),
    ]
    for k in range(len(dirs[0][0])):
        for stages, _, _, _ in dirs:
            for c in range(CHUNKS_PER_TD):
                stages[k](c)
    for c in range(CHUNKS_PER_TD):
        for _, rec_first, _, _ in dirs:
            rec_first(c)
        for _, _, rec_second, _ in dirs:
            rec_second(c)

    for d, (_, _, _, carry) in enumerate(dirs):
        for hd in range(H_A):
            s_scr[d, hd] = carry["state"][hd]

    @pl.when(is_ctx)
    def _():
        for d, (_, _, _, carry) in enumerate(dirs):
            first, second = (1, 0) if d == 1 else (0, 1)
            for hd in range(H_A):
                ns_ref[first, d, hd] = carry["mid"][hd]
                ns_ref[second, d, hd] = carry["state"][hd]


def _delta_call(layer, qkv, gb, state_delta, new_state):
    def tile_specs(rev):
        def spec(width, col_block):
            return pl.BlockSpec((TD, width), lambda t: (_delta_tile_row(rev, t), col_block))
        return [spec(D_A, 0), spec(D_A, 1), spec(D_A, 2), spec(LANES, 0)]

    def s0_index(t):
        b = jnp.maximum(t - CTX_DTILES, 0) // LAT_DTILES_PER_SEQ
        return (b, layer, 0, 0, 0, 0)

    def ns_index(t):
        return (jnp.minimum(t, CTX_DTILES - 1), layer, 0, 0, 0, 0)

    return pl.pallas_call(
        _delta_kernel,
        grid=(N_DTILES,),
        in_specs=tile_specs(False) + tile_specs(True) + [
            pl.BlockSpec((None, None, 2, H_A, DK, DV), s0_index),
            pl.BlockSpec(memory_space=pl.ANY),
        ],
        out_specs=[
            pl.BlockSpec((TD, D_A), lambda t: (_delta_tile_row(False, t), 0)),
            pl.BlockSpec((TD, D_A), lambda t: (_delta_tile_row(True, t), 0)),
            pl.BlockSpec((SEQ_PER_CTX_DTILE, None, 2, H_A, DK, DV), ns_index),
        ],
        out_shape=[
            jax.ShapeDtypeStruct((N_TOK, D_A), F32),
            jax.ShapeDtypeStruct((N_TOK, D_A), F32),
            jax.ShapeDtypeStruct(new_state.shape, F32),
        ],
        scratch_shapes=[pltpu.VMEM((2, H_A, DK, DV), F32)],
        input_output_aliases={9: 2},
        compiler_params=pltpu.CompilerParams(
            dimension_semantics=("arbitrary",), vmem_limit_bytes=VMEM_LIMIT_BYTES),
        name=f"delta_l{layer}",
    )(qkv, qkv, qkv, gb, qkv, qkv, qkv, gb, state_delta, new_state)


def _post_kernel(final, split, *refs):
    is_ctx = pl.program_id(0) < CTX_TILES
    (of_ref, ob_ref, rest_ref, mod_ref, onorm_ref, ws_ref, bs_ref, wout_ref, nffn_ref, wff1_ref,
     wff2_ref, nfin_ref) = refs[1 + split:13 + split]
    out_refs = refs[13 + split:]
    m = mod_ref[0]
    ga1 = m[:, 2 * D_MODEL:3 * D_MODEL]
    sh2 = m[:, 3 * D_MODEL:4 * D_MODEL]
    coef2 = nffn_ref[...] * (1.0 + m[:, 4 * D_MODEL:5 * D_MODEL])
    ga2 = m[:, 5 * D_MODEL:6 * D_MODEL]
    onw = onorm_ref[...]
    bs = bs_ref[...]

    def head(rows):
        if split:
            x = jnp.where(is_ctx, refs[0][rows, :], refs[1][rows, :])
        else:
            x = refs[0][rows, :]
        o = of_ref[rows, :] + ob_ref[rows, :]
        mix_in = []
        for hd in range(H_A):
            oh = o[:, hd * DV:(hd + 1) * DV]
            oh = oh * (lax.rsqrt(jnp.mean(oh * oh, axis=-1, keepdims=True) + EPS) * onw)
            gate = _silu(rest_ref[rows, hd * DV:(hd + 1) * DV].astype(F32))
            mix_in.append((oh * gate).astype(BF16))
        for g in range(G_B):
            ug = _gelu_tanh(rest_ref[rows, D_A + g * C_B:D_A + (g + 1) * C_B].astype(F32))
            vg = _gelu_tanh(rest_ref[rows, 2 * D_A + g * C_B:2 * D_A + (g + 1) * C_B].astype(F32))
            dv = vg - jnp.mean(vg, axis=-1, keepdims=True)
            var = jnp.mean(dv * dv, axis=-1, keepdims=True)
            vgn = (dv * lax.rsqrt(var + EPS)).astype(BF16)
            parts = []
            for c in range(POST_SUB // CHUNK_MLP):
                crows = slice(c * CHUNK_MLP, (c + 1) * CHUNK_MLP)
                mixed = _dot(ws_ref[g], vgn[crows]) + bs[:, g:g + 1]
                parts.append((ug[crows] * mixed).astype(BF16))
            mix_in.append(jnp.concatenate(parts, axis=0))
        mix = jnp.concatenate(mix_in, axis=-1)
        x1 = x + ga1 * _dot(mix, wout_ref[...])
        ms = jnp.mean(x1 * x1, axis=-1, keepdims=True)
        return x1, ((x1 * lax.rsqrt(ms + EPS)) * coef2 + sh2).astype(BF16)

    subs = [slice(r0, r0 + POST_SUB) for r0 in range(0, TM, POST_SUB)]
    ys = []
    nxt = head(subs[0])
    for s, rows in enumerate(subs):
        x1, h2 = nxt
        a = jnp.maximum(_dot(h2, wff1_ref[...]), 0.0)
        if s + 1 < len(subs):
            nxt = head(subs[s + 1])
        x2 = x1 + ga2 * _dot((a * a).astype(BF16), wff2_ref[...])
        if final:
            ms2 = jnp.mean(x2 * x2, axis=-1, keepdims=True)
            x2 = (x2 * lax.rsqrt(ms2 + EPS)) * nfin_ref[...]
        ys.append((rows, x2))

    if not final:
        for rows, y in ys:
            out_refs[0][rows, :] = y
        return
    yc_ref, yl_ref = out_refs

    @pl.when(is_ctx)
    def _():
        for rows, y in ys:
            yc_ref[rows, :] = y

    @pl.when(jnp.logical_not(is_ctx))
    def _():
        for rows, y in ys:
            yl_ref[rows, :] = y


def _post_call(layer, final, xs, o_fwd, o_bwd, rest, mod, o_norm, ws, bs_t, w_out, norm_ffn, w_ff1,
               w_ff2, norm_final):
    split = len(xs) == 2
    const = dict(pipeline_mode=pl.Buffered(1))
    if split:
        x_specs = [pl.BlockSpec((TM, D_MODEL), lambda i: (jnp.minimum(i, CTX_TILES - 1), 0)),
                   pl.BlockSpec((TM, D_MODEL), lambda i: (jnp.maximum(i - CTX_TILES, 0), 0))]
    else:
        x_specs = [pl.BlockSpec((TM, D_MODEL), lambda i: (i, 0))]
    if final:
        out_specs = [
            pl.BlockSpec((TM, D_MODEL), lambda i: (jnp.minimum(i, CTX_TILES - 1), 0)),
            pl.BlockSpec((TM, D_MODEL), lambda i: (jnp.maximum(i - CTX_TILES, 0), 0)),
        ]
        out_shape = [jax.ShapeDtypeStruct((N_CTX, D_MODEL), F32),
                     jax.ShapeDtypeStruct((N_LAT, D_MODEL), F32)]
    else:
        out_specs = pl.BlockSpec((TM, D_MODEL), lambda i: (i, 0))
        out_shape = jax.ShapeDtypeStruct((N_TOK, D_MODEL), F32)
    return pl.pallas_call(
        functools.partial(_post_kernel, final, split),
        grid=(N_TILES,),
        in_specs=x_specs + [
            pl.BlockSpec((TM, D_A), lambda i: (i, 0)),
            pl.BlockSpec((TM, D_A), lambda i: (i, 0)),
            pl.BlockSpec((TM, 3 * D_A), lambda i: (i, 0)),
            pl.BlockSpec((None, 1, 1, N_MOD * D_MODEL), lambda i: (layer, _mod_row(i), 0, 0)),
            pl.BlockSpec((None, 1, DV), lambda i: (layer, 0, 0), **const),
            pl.BlockSpec((None, G_B, CHUNK_MLP, CHUNK_MLP), lambda i: (layer, 0, 0, 0), **const),
            pl.BlockSpec((None, CHUNK_MLP, LANES), lambda i: (layer, 0, 0), **const),
            pl.BlockSpec((None, D_MODEL, D_MODEL), lambda i: (layer, 0, 0), **const),
            pl.BlockSpec((None, 1, D_MODEL), lambda i: (layer, 0, 0), **const),
            pl.BlockSpec((None, D_MODEL, D_FF), lambda i: (layer, 0, 0), **const),
            pl.BlockSpec((None, D_FF, D_MODEL), lambda i: (layer, 0, 0), **const),
            pl.BlockSpec((1, D_MODEL), lambda i: (0, 0), **const),
        ],
        out_specs=out_specs,
        out_shape=out_shape,
        compiler_params=pltpu.CompilerParams(
            dimension_semantics=("arbitrary",), vmem_limit_bytes=VMEM_LIMIT_BYTES),
        name=f"post_l{layer}",
    )(*xs, o_fwd, o_bwd, rest, mod, o_norm, ws, bs_t, w_out, norm_ffn, w_ff1, w_ff2, norm_final)


def kernel(x_prompt, x_sample, state_delta, c, c_ctx, w_mod, b_mod, norm_mix, w_in, conv_qkv, a_log,
           dt_bias, o_norm, w_spatial, b_spatial, w_out, norm_ffn, w_ff1, w_ff2, norm_final):
    cond8 = jnp.concatenate([c_ctx[None, :], c, jnp.zeros((8 - 1 - LAT_B, D_MODEL), F32)], axis=0)

    n_qkvg = 4 * D_A
    w_main = jnp.concatenate([w_in[:, :, :n_qkvg], w_in[:, :, n_qkvg + 4 * H_A:]], axis=-1).astype(BF16)
    w_ab = jnp.pad(w_in[:, :, n_qkvg:n_qkvg + 4 * H_A], ((0, 0), (0, 0), (0, LANES - 4 * H_A))).astype(BF16)
    a_row = jnp.pad(a_log.reshape(DEPTH, 1, 2 * H_A), ((0, 0), (0, 0), (2 * H_A, LANES - 4 * H_A)))
    dt_row = jnp.pad(dt_bias.reshape(DEPTH, 1, 2 * H_A), ((0, 0), (0, 0), (2 * H_A, LANES - 4 * H_A)))
    bs_t = jnp.pad(jnp.swapaxes(b_spatial, 1, 2), ((0, 0), (0, 0), (0, LANES - G_B)))
    ws_bf = w_spatial.astype(BF16)
    w_out_bf = w_out.astype(BF16)
    w_ff1_bf = w_ff1.astype(BF16)
    w_ff2_bf = w_ff2.astype(BF16)
    norm_mix3 = norm_mix.reshape(DEPTH, 1, D_MODEL)
    norm_ffn3 = norm_ffn.reshape(DEPTH, 1, D_MODEL)
    o_norm3 = o_norm.reshape(DEPTH, 1, DV)
    norm_final2 = norm_final.reshape(1, D_MODEL)

    mod = _modulation(cond8, w_mod, b_mod).reshape(DEPTH, 8, 1, N_MOD * D_MODEL)

    xs = (x_prompt.reshape(N_CTX, D_MODEL), x_sample.reshape(N_LAT, D_MODEL))
    new_state = jnp.zeros((CTX_B, DEPTH, 2, H_A, DK, DV), F32)
    for layer in range(DEPTH):
        qkv, rest, gb = _pre_call(layer, xs, mod, norm_mix3, w_main, w_ab, conv_qkv, a_row, dt_row)
        o_fwd, o_bwd, new_state = _delta_call(layer, qkv, gb, state_delta, new_state)
        out = _post_call(layer, layer == DEPTH - 1, xs, o_fwd, o_bwd, rest, mod, o_norm3, ws_bf, bs_t,
                         w_out_bf, norm_ffn3, w_ff1_bf, w_ff2_bf, norm_final2)
        xs = (out,)

    y_prompt, y_sample = out
    return (y_prompt.reshape(CTX_B, CTX_T, D_MODEL), y_sample.reshape(LAT_B, LAT_T, D_MODEL), new_state)
```

```python
import functools

import jax
import jax.numpy as jnp
from jax import lax
from jax.experimental import pallas as pl
from jax.experimental.pallas import tpu as pltpu

F32 = jnp.float32
BF16 = jnp.bfloat16

D_MODEL = 1024
DEPTH = 4
N_MOD = 6
H_A = 4
DK = 128
DV = 128
D_A = H_A * DK
G_B = 4
C_B = 128
D_B = G_B * C_B
CHUNK_MLP = 128
D_FF = 4 * D_MODEL
EPS = 1e-6

CTX_B, CTX_T = 32, 256
LAT_B, LAT_T = 2, 4096
N_CTX = CTX_B * CTX_T
N_LAT = LAT_B * LAT_T
N_TOK = N_CTX + N_LAT

LANES = 128
MOD_TN = 3072
TM = 512
N_TILES = N_TOK // TM
CTX_TILES = N_CTX // TM
LAT_TILES_PER_SEQ = LAT_T // TM
HALO = 8
POST_SUB = 256

CHUNK = 128
INV_BASE = 8
BF16_TILE_ROWS = 16
TD = 512
CHUNKS_PER_TD = TD // CHUNK
SEQ_PER_CTX_DTILE = TD // CTX_T
CTX_DTILES = N_CTX // TD
LAT_DTILES_PER_SEQ = LAT_T // TD
N_DTILES = N_TOK // TD

N_FEAT = 6 * D_A
VMEM_LIMIT_BYTES = 60000 * 1024


def _sigmoid(x):
    return 1.0 / (1.0 + jnp.exp(-x))


def _silu(x):
    hx = 0.5 * x
    return hx + hx * jnp.tanh(hx)


_GELU_C0 = 0.7978845608028654
_GELU_C1 = _GELU_C0 * 0.044715


def _gelu_tanh(x):
    hx = 0.5 * x
    return hx + hx * jnp.tanh(x * (_GELU_C0 + _GELU_C1 * (x * x)))


def _softplus(x):
    return jnp.maximum(x, 0.0) + jnp.log1p(jnp.exp(-jnp.abs(x)))


def _dot(a, b):
    return jnp.dot(a, b, preferred_element_type=F32)


def _dot_nt(a, b):
    return lax.dot_general(a, b, (((1,), (1,)), ((), ())), preferred_element_type=F32)


def _dot_tn(a, b):
    return lax.dot_general(a, b, (((0,), (0,)), ((), ())), preferred_element_type=F32)


def _mod_kernel(cond_ref, w_ref, b_ref, o_ref):
    c = cond_ref[...]
    s = (c * _sigmoid(c)).astype(BF16)
    o_ref[0] = _dot(s, w_ref[0].astype(BF16)) + b_ref[0]


def _modulation(cond8, w_mod, b_mod):
    tn = MOD_TN
    return pl.pallas_call(
        _mod_kernel,
        grid=(DEPTH, (N_MOD * D_MODEL) // tn),
        in_specs=[
            pl.BlockSpec((8, D_MODEL), lambda l, j: (0, 0)),
            pl.BlockSpec((1, D_MODEL, tn), lambda l, j: (l, 0, j)),
            pl.BlockSpec((1, 1, tn), lambda l, j: (l, 0, j)),
        ],
        out_specs=pl.BlockSpec((1, 8, tn), lambda l, j: (l, 0, j)),
        out_shape=jax.ShapeDtypeStruct((DEPTH, 8, N_MOD * D_MODEL), F32),
        compiler_params=pltpu.CompilerParams(
            dimension_semantics=("arbitrary", "arbitrary"), vmem_limit_bytes=VMEM_LIMIT_BYTES),
        name="modulation",
    )(cond8, w_mod, b_mod.reshape(DEPTH, 1, N_MOD * D_MODEL))


def _mod_row(i):
    return jnp.where(i < CTX_TILES, 0, 1 + (i - CTX_TILES) // LAT_TILES_PER_SEQ)


def _pre_kernel(split, *refs):
    i = pl.program_id(0)
    is_ctx = i < CTX_TILES
    if split:
        (xc_ref, xl_ref, xcp_ref, xlp_ref, xcn_ref, xln_ref, mod_ref, nw_ref, w_ref, wab_ref,
         conv_ref, arow_ref, dtrow_ref, qkv_ref, rest_ref, gb_ref) = refs
        x = jnp.where(is_ctx, xc_ref[...], xl_ref[...])
        x_prev = jnp.where(is_ctx, xcp_ref[...], xlp_ref[...])
        x_next = jnp.where(is_ctx, xcn_ref[...], xln_ref[...])
    else:
        (x_ref, xp_ref, xn_ref, mod_ref, nw_ref, w_ref, wab_ref,
         conv_ref, arow_ref, dtrow_ref, qkv_ref, rest_ref, gb_ref) = refs
        x, x_prev, x_next = x_ref[...], xp_ref[...], xn_ref[...]
    m = mod_ref[0]
    sh1 = m[:, 0:D_MODEL]
    coef = nw_ref[...] * (1.0 + m[:, D_MODEL:2 * D_MODEL])

    def norm_mod(xx):
        ms = jnp.mean(xx * xx, axis=-1, keepdims=True)
        return ((xx * lax.rsqrt(ms + EPS)) * coef + sh1).astype(BF16)

    h = norm_mod(x)
    hh = norm_mod(jnp.concatenate([x_prev, x_next], axis=0))
    h_ext = jnp.concatenate([h, hh], axis=0)

    lat_j = (i - CTX_TILES) % LAT_TILES_PER_SEQ
    starts_seq = jnp.logical_or(is_ctx, lat_j == 0)
    ends_seq = jnp.logical_or(is_ctx, lat_j == LAT_TILES_PER_SEQ - 1)
    r8 = lax.broadcasted_iota(jnp.int32, (HALO, 1), 0)

    def shifted(p, prev_row, next_row):
        dn = pltpu.roll(p, 1, 0)
        up = pltpu.roll(p, TM - 1, 0)
        dn_parts, up_parts = [], []
        for b in range(0, TM, CTX_T):
            slab = dn[b:b + HALO]
            if b == 0:
                slab = jnp.where(r8 == 0, jnp.where(starts_seq, 0.0, prev_row), slab)
            else:
                slab = jnp.where(jnp.logical_and(r8 == 0, is_ctx), 0.0, slab)
            dn_parts += [slab, dn[b + HALO:b + CTX_T]]
            e = b + CTX_T
            slab = up[e - HALO:e]
            if e == TM:
                slab = jnp.where(r8 == HALO - 1, jnp.where(ends_seq, 0.0, next_row), slab)
            else:
                slab = jnp.where(jnp.logical_and(r8 == HALO - 1, is_ctx), 0.0, slab)
            up_parts += [up[b:e - HALO], slab]
        return jnp.concatenate(dn_parts, axis=0), jnp.concatenate(up_parts, axis=0)

    cw = conv_ref[...]
    for blk in range(3):
        cols = slice(blk * D_A, (blk + 1) * D_A)
        pe = _dot(h_ext, w_ref[:, cols])
        rcols = slice((3 + blk) * D_A, (4 + blk) * D_A)
        rest_ref[:, blk * D_A:(blk + 1) * D_A] = _dot(h, w_ref[:, rcols]).astype(BF16)
        if blk == 0:
            ab = _dot(h, wab_ref[...])
            lane = lax.broadcasted_iota(jnp.int32, (TM, LANES), 1)
            g_log = -jnp.exp(arow_ref[...]) * _softplus(ab + dtrow_ref[...])
            gb_ref[...] = jnp.where(lane < 2 * H_A, _sigmoid(ab), g_log)
        p = pe[0:TM]
        xm1, xp1 = shifted(p, pe[TM + HALO - 1:TM + HALO], pe[TM + HALO:TM + HALO + 1])
        y = _silu(cw[0:1, cols] * xm1 + cw[1:2, cols] * p + cw[2:3, cols] * xp1)
        if blk < 2:
            scale = DK ** -0.5 if blk == 0 else 1.0
            for hd in range(H_A):
                yh = y[:, hd * DK:(hd + 1) * DK]
                inv_norm = lax.rsqrt(jnp.sum(yh * yh, axis=-1, keepdims=True) + EPS) * scale
                qkv_ref[:, blk * D_A + hd * DK:blk * D_A + (hd + 1) * DK] = (yh * inv_norm).astype(BF16)
        else:
            qkv_ref[:, cols] = y.astype(BF16)


def _pre_call(layer, xs, mod, norm_mix, w_main, w_ab, conv_qkv, a_row, dt_row):
    split = len(xs) == 2
    bpt = TM // HALO
    const = dict(pipeline_mode=pl.Buffered(1))

    def tile_specs(first_tile, n_tiles):
        def local(i):
            return jnp.clip(i - first_tile, 0, n_tiles - 1)
        return (
            pl.BlockSpec((TM, D_MODEL), lambda i: (local(i), 0)),
            pl.BlockSpec((HALO, D_MODEL), lambda i: (jnp.maximum(local(i) * bpt - 1, 0), 0)),
            pl.BlockSpec((HALO, D_MODEL),
                         lambda i: (jnp.minimum((local(i) + 1) * bpt, n_tiles * bpt - 1), 0)),
        )

    if split:
        c_specs = tile_specs(0, CTX_TILES)
        l_specs = tile_specs(CTX_TILES, N_TILES - CTX_TILES)
        x_specs = [c_specs[0], l_specs[0], c_specs[1], l_specs[1], c_specs[2], l_specs[2]]
        x_args = [xs[0], xs[1]] * 3
    else:
        x_specs = list(tile_specs(0, N_TILES))
        x_args = [xs[0]] * 3
    return pl.pallas_call(
        functools.partial(_pre_kernel, split),
        grid=(N_TILES,),
        in_specs=x_specs + [
            pl.BlockSpec((None, 1, 1, N_MOD * D_MODEL), lambda i: (layer, _mod_row(i), 0, 0)),
            pl.BlockSpec((None, 1, D_MODEL), lambda i: (layer, 0, 0), **const),
            pl.BlockSpec((None, D_MODEL, N_FEAT), lambda i: (layer, 0, 0), **const),
            pl.BlockSpec((None, D_MODEL, LANES), lambda i: (layer, 0, 0), **const),
            pl.BlockSpec((None, 3, 3 * D_A), lambda i: (layer, 0, 0), **const),
            pl.BlockSpec((None, 1, LANES), lambda i: (layer, 0, 0), **const),
            pl.BlockSpec((None, 1, LANES), lambda i: (layer, 0, 0), **const),
        ],
        out_specs=[
            pl.BlockSpec((TM, 3 * D_A), lambda i: (i, 0)),
            pl.BlockSpec((TM, 3 * D_A), lambda i: (i, 0)),
            pl.BlockSpec((TM, LANES), lambda i: (i, 0)),
        ],
        out_shape=[
            jax.ShapeDtypeStruct((N_TOK, 3 * D_A), BF16),
            jax.ShapeDtypeStruct((N_TOK, 3 * D_A), BF16),
            jax.ShapeDtypeStruct((N_TOK, LANES), F32),
        ],
        compiler_params=pltpu.CompilerParams(
            dimension_semantics=("arbitrary",), vmem_limit_bytes=VMEM_LIMIT_BYTES),
        name=f"pre_l{layer}",
    )(*x_args, mod, norm_mix, w_main, w_ab, conv_qkv, a_row, dt_row)


def _delta_tile_row(rev, t):
    if not rev:
        return t
    lat = t - CTX_DTILES
    b = lat // LAT_DTILES_PER_SEQ
    j = LAT_DTILES_PER_SEQ - 1 - lat % LAT_DTILES_PER_SEQ
    return jnp.where(t < CTX_DTILES, t, CTX_DTILES + b * LAT_DTILES_PER_SEQ + j)


def _mm_inv(a, b):
    return _dot(a.astype(BF16), b.astype(BF16))


def _row_groups(x, size):
    return [x[g * size:(g + 1) * size] for g in range(x.shape[0] // size)]


def _delta_direction(rev, is_ctx, q_ref, k_ref, v_ref, gb_ref, o_ref, state0):
    row = lax.broadcasted_iota(jnp.int32, (CHUNK, CHUNK), 0)
    col = lax.broadcasted_iota(jnp.int32, (CHUNK, CHUNK), 1)
    incl = row <= col if rev else row >= col
    strict = row < col if rev else row > col
    eye = (row == col).astype(F32)
    base_mask = (row // INV_BASE) == (col // INV_BASE)
    levels = []
    blk = INV_BASE
    while blk < CHUNK:
        levels.append((blk, jnp.logical_and((row // (2 * blk)) == (col // (2 * blk)),
                                            (row // blk) != (col // blk))))
        blk *= 2
    rix = lax.broadcasted_iota(jnp.int32, (CHUNK, LANES), 0)
    lane_beta = H_A if rev else 0
    lane_g = 3 * H_A if rev else 2 * H_A
    last = 0 if rev else CHUNK - 1

    chunk_order = range(CHUNKS_PER_TD - 1, -1, -1) if rev else range(CHUNKS_PER_TD)
    chunk_rows = [slice(cc * CHUNK, (cc + 1) * CHUNK) for cc in chunk_order]
    probs = [[dict(hd=hd, hcols=slice(hd * DK, (hd + 1) * DK)) for hd in range(H_A)]
             for _ in range(CHUNKS_PER_TD)]
    carry = dict(state=list(state0), mid=None)

    def prep(c):
        rows = chunk_rows[c]
        gb = gb_ref[rows, :]
        fwd_sum = gb
        s = 1
        while s < CHUNK:
            fwd_sum = fwd_sum + jnp.where(rix >= s, pltpu.roll(fwd_sum, s, 0), 0.0)
            s *= 2
        cum = fwd_sum[CHUNK - 1:CHUNK, :] - fwd_sum + gb if rev else fwd_sum
        cum_t = cum.T
        for p in probs[c]:
            hd = p["hd"]
            g_b = jnp.broadcast_to(cum[:, lane_g + hd:lane_g + hd + 1], (CHUNK, CHUNK))
            p.update(beta_b=jnp.broadcast_to(gb[:, lane_beta + hd:lane_beta + hd + 1], (CHUNK, CHUNK)),
                     g_b=g_b,
                     g_row=cum_t[lane_g + hd:lane_g + hd + 1, :],
                     g_tot=g_b[last:last + 1, :])

    def kq_stage(c):
        rows = chunk_rows[c]
        for p in probs[c]:
            k_bf = k_ref[rows, p["hcols"]]
            kb = (k_bf.astype(F32) * p["beta_b"]).astype(BF16)
            kq = _dot_nt(jnp.concatenate([kb, q_ref[rows, p["hcols"]]], axis=0), k_bf)
            decay = jnp.exp(jnp.minimum(p["g_b"] - p["g_row"], 0.0))
            p["a"] = jnp.where(strict, kq[0:CHUNK] * decay, 0.0)
            p["qk"] = jnp.where(incl, kq[CHUNK:2 * CHUNK] * decay, 0.0).astype(BF16)

    def base_first(c):
        for p in probs[c]:
            a0 = jnp.where(base_mask, p["a"], 0.0)
            p["inv"] = eye - a0
            p["pw"] = _mm_inv(a0, a0).astype(BF16)

    def base_mid(c):
        for p in probs[c]:
            st = _dot(jnp.concatenate([p["pw"], p["inv"].astype(BF16)], axis=0), p["pw"])
            p["inv"] = p["inv"] + st[CHUNK:2 * CHUNK]
            p["pw"] = st[0:CHUNK].astype(BF16)

    def base_last(c):
        for p in probs[c]:
            p["inv"] = p["inv"] + _dot(p["inv"].astype(BF16), p["pw"])

    upd_parity = 0 if rev else 1

    def merge_first(size, off_mask, c):
        for p in probs[c]:
            p["groups"] = _row_groups(p["inv"], size)
            lhs = jnp.concatenate(p["groups"][upd_parity::2], axis=0)
            p["tmp"] = _mm_inv(lhs, jnp.where(off_mask, p["a"], 0.0))

    def merge_second(size, c):
        for p in probs[c]:
            upd = _row_groups(_mm_inv(p["tmp"], p["inv"]), size)
            new = [g - upd[i // 2] if i % 2 == upd_parity else g for i, g in enumerate(p["groups"])]
            p["inv"] = jnp.concatenate(new, axis=0)

    def to_bf16(c):
        for p in probs[c]:
            p["inv"] = p["inv"].astype(BF16)
            p["a"] = p["a"].astype(BF16)

    def merge_first_bf16(size, mask_bf, c):
        for p in probs[c]:
            p["groups"] = _row_groups(p["inv"], size)
            lhs = jnp.concatenate(p["groups"][upd_parity::2], axis=0)
            p["tmp"] = _dot(lhs, p["a"] * mask_bf).astype(BF16)

    def merge_second_bf16(size, c):
        for p in probs[c]:
            upd = _row_groups(_dot(p["tmp"], p["inv"]), size)
            new = [(g.astype(F32) - upd[i // 2]).astype(BF16) if i % 2 == upd_parity else g
                   for i, g in enumerate(p["groups"])]
            p["inv"] = jnp.concatenate(new, axis=0)

    def solve(c):
        rows = chunk_rows[c]
        for p in probs[c]:
            beta_b = p["beta_b"]
            rhs = jnp.concatenate(
                [v_ref[rows, p["hcols"]].astype(F32) * beta_b,
                 k_ref[rows, p["hcols"]].astype(F32) * beta_b * jnp.exp(p["g_b"])], axis=-1)
            sol = _dot(p["inv"].astype(BF16), rhs.astype(BF16))
            p["u"] = sol[:, 0:DV]
            p["w"] = sol[:, DV:2 * DV].astype(BF16)

    def rec_first(c):
        if c == CHUNKS_PER_TD // 2:
            carry["mid"] = carry["state"]
            carry["state"] = [jnp.where(is_ctx, 0.0, st) for st in carry["state"]]
        for p in probs[c]:
            p["st_bf"] = carry["state"][p["hd"]].astype(BF16)
            p["ws"] = _dot(p["w"], p["st_bf"])

    def rec_second(c):
        state = list(carry["state"])
        rows = chunk_rows[c]
        for p in probs[c]:
            g_b, g_tot = p["g_b"], p["g_tot"]
            q_dec = (q_ref[rows, p["hcols"]].astype(F32) * jnp.exp(g_b)).astype(BF16)
            k_dec = (k_ref[rows, p["hcols"]].astype(F32) * jnp.exp(g_tot - g_b)).astype(BF16)
            v_new = (p["u"] - p["ws"]).astype(BF16)
            o_ref[rows, p["hcols"]] = _dot(jnp.concatenate([q_dec, p["qk"]], axis=1),
                                           jnp.concatenate([p["st_bf"], v_new], axis=0))
            state[p["hd"]] = state[p["hd"]] * jnp.exp(g_tot) + _dot_tn(k_dec, v_new)
        carry["state"] = state

    stages = [prep, kq_stage, base_first]
    n = 2
    while 2 * n < INV_BASE:
        stages.append(base_mid)
        n *= 2
    stages.append(base_last)
    in_bf16 = False
    for size, off_mask in levels:
        if size % BF16_TILE_ROWS == 0 and not in_bf16:
            stages.append(to_bf16)
            in_bf16 = True
        if in_bf16:
            mask_bf = jnp.where(off_mask, 1.0, 0.0).astype(BF16)
            stages.append(functools.partial(merge_first_bf16, size, mask_bf))
            stages.append(functools.partial(merge_second_bf16, size))
        else:
            stages.append(functools.partial(merge_first, size, off_mask))
            stages.append(functools.partial(merge_second, size))
    stages.append(solve)
    return stages, rec_first, rec_second, carry


def _delta_kernel(qf_ref, kf_ref, vf_ref, gbf_ref, qb_ref, kb_ref, vb_ref, gbb_ref, s0_ref,
                  wout_ref, wff1_ref, wff2_ref, ns_in_ref,
                  of_ref, ob_ref, ns_ref, wout_bf_ref, wff1_bf_ref, wff2_bf_ref, s_scr):
    del ns_in_ref
    wout_bf_ref[...] = wout_ref[...].astype(BF16)
    wff1_bf_ref[...] = wff1_ref[...].astype(BF16)
    wff2_bf_ref[...] = wff2_ref[...].astype(BF16)
    t = pl.program_id(0)
    is_ctx = t < CTX_DTILES

    @pl.when(is_ctx)
    def _():
        s_scr[...] = jnp.zeros_like(s_scr)

    @pl.when(jnp.logical_and(t >= CTX_DTILES, (t - CTX_DTILES) % LAT_DTILES_PER_SEQ == 0))
    def _():
        s_scr[...] = s0_ref[...]

    dirs = [
        _delta_direction(False, is_ctx, qf_ref, kf_ref, vf_ref, gbf_ref, of_ref,
                         [s_scr[0, hd] for hd in range(H_A)]),
        _delta_direction(True, is_ctx, qb_ref, kb_ref, vb_ref, gbb_ref, ob_ref,
                         [s_scr[1, hd] for hd in range(H_A)]),
    ]
    for k in range(len(dirs[0][0])):
        for stages, _, _, _ in dirs:
            for c in range(CHUNKS_PER_TD):
                stages[k](c)
    for c in range(CHUNKS_PER_TD):
        for _, rec_first, _, _ in dirs:
            rec_first(c)
        for _, _, rec_second, _ in dirs:
            rec_second(c)

    for d, (_, _, _, carry) in enumerate(dirs):
        for hd in range(H_A):
            s_scr[d, hd] = carry["state"][hd]

    @pl.when(is_ctx)
    def _():
        for d, (_, _, _, carry) in enumerate(dirs):
            first, second = (1, 0) if d == 1 else (0, 1)
            for hd in range(H_A):
                ns_ref[first, d, hd] = carry["mid"][hd]
                ns_ref[second, d, hd] = carry["state"][hd]


def _delta_call(layer, qkv, gb, state_delta, new_state, w_out, w_ff1, w_ff2):
    wo_rows, f1_cols, f2_rows = D_MODEL // N_DTILES, D_FF // N_DTILES, D_FF // N_DTILES

    def tile_specs(rev):
        def spec(width, col_block):
            return pl.BlockSpec((TD, width), lambda t: (_delta_tile_row(rev, t), col_block))
        return [spec(D_A, 0), spec(D_A, 1), spec(D_A, 2), spec(LANES, 0)]

    def s0_index(t):
        b = jnp.maximum(t - CTX_DTILES, 0) // LAT_DTILES_PER_SEQ
        return (b, layer, 0, 0, 0, 0)

    def ns_index(t):
        return (jnp.minimum(t, CTX_DTILES - 1), layer, 0, 0, 0, 0)

    return pl.pallas_call(
        _delta_kernel,
        grid=(N_DTILES,),
        in_specs=tile_specs(False) + tile_specs(True) + [
            pl.BlockSpec((None, None, 2, H_A, DK, DV), s0_index),
            pl.BlockSpec((None, wo_rows, D_MODEL), lambda t: (layer, t, 0)),
            pl.BlockSpec((None, D_MODEL, f1_cols), lambda t: (layer, 0, t)),
            pl.BlockSpec((None, f2_rows, D_MODEL), lambda t: (layer, t, 0)),
            pl.BlockSpec(memory_space=pl.ANY),
        ],
        out_specs=[
            pl.BlockSpec((TD, D_A), lambda t: (_delta_tile_row(False, t), 0)),
            pl.BlockSpec((TD, D_A), lambda t: (_delta_tile_row(True, t), 0)),
            pl.BlockSpec((SEQ_PER_CTX_DTILE, None, 2, H_A, DK, DV), ns_index),
            pl.BlockSpec((wo_rows, D_MODEL), lambda t: (t, 0)),
            pl.BlockSpec((D_MODEL, f1_cols), lambda t: (0, t)),
            pl.BlockSpec((f2_rows, D_MODEL), lambda t: (t, 0)),
        ],
        out_shape=[
            jax.ShapeDtypeStruct((N_TOK, D_A), F32),
            jax.ShapeDtypeStruct((N_TOK, D_A), F32),
            jax.ShapeDtypeStruct(new_state.shape, F32),
            jax.ShapeDtypeStruct((D_MODEL, D_MODEL), BF16),
            jax.ShapeDtypeStruct((D_MODEL, D_FF), BF16),
            jax.ShapeDtypeStruct((D_FF, D_MODEL), BF16),
        ],
        scratch_shapes=[pltpu.VMEM((2, H_A, DK, DV), F32)],
        input_output_aliases={12: 2},
        compiler_params=pltpu.CompilerParams(
            dimension_semantics=("arbitrary",), vmem_limit_bytes=VMEM_LIMIT_BYTES),
        name=f"delta_l{layer}",
    )(qkv, qkv, qkv, gb, qkv, qkv, qkv, gb, state_delta, w_out, w_ff1, w_ff2, new_state)


def _post_kernel(final, split, *refs):
    is_ctx = pl.program_id(0) < CTX_TILES
    (of_ref, ob_ref, rest_ref, mod_ref, onorm_ref, ws_ref, bs_ref, wout_ref, nffn_ref, wff1_ref,
     wff2_ref, nfin_ref) = refs[1 + split:13 + split]
    out_refs = refs[13 + split:]
    m = mod_ref[0]
    ga1 = m[:, 2 * D_MODEL:3 * D_MODEL]
    sh2 = m[:, 3 * D_MODEL:4 * D_MODEL]
    coef2 = nffn_ref[...] * (1.0 + m[:, 4 * D_MODEL:5 * D_MODEL])
    ga2 = m[:, 5 * D_MODEL:6 * D_MODEL]
    onw = onorm_ref[...]
    bs = bs_ref[...]

    def head(rows):
        if split:
            x = jnp.where(is_ctx, refs[0][rows, :], refs[1][rows, :])
        else:
            x = refs[0][rows, :]
        o = of_ref[rows, :] + ob_ref[rows, :]
        mix_in = []
        for hd in range(H_A):
            oh = o[:, hd * DV:(hd + 1) * DV]
            oh = oh * (lax.rsqrt(jnp.mean(oh * oh, axis=-1, keepdims=True) + EPS) * onw)
            gate = _silu(rest_ref[rows, hd * DV:(hd + 1) * DV].astype(F32))
            mix_in.append((oh * gate).astype(BF16))
        for g in range(G_B):
            ug = _gelu_tanh(rest_ref[rows, D_A + g * C_B:D_A + (g + 1) * C_B].astype(F32))
            vg = _gelu_tanh(rest_ref[rows, 2 * D_A + g * C_B:2 * D_A + (g + 1) * C_B].astype(F32))
            dv = vg - jnp.mean(vg, axis=-1, keepdims=True)
            var = jnp.mean(dv * dv, axis=-1, keepdims=True)
            vgn = (dv * lax.rsqrt(var + EPS)).astype(BF16)
            parts = []
            for c in range(POST_SUB // CHUNK_MLP):
                crows = slice(c * CHUNK_MLP, (c + 1) * CHUNK_MLP)
                mixed = _dot(ws_ref[g], vgn[crows]) + bs[:, g:g + 1]
                parts.append((ug[crows] * mixed).astype(BF16))
            mix_in.append(jnp.concatenate(parts, axis=0))
        mix = jnp.concatenate(mix_in, axis=-1)
        x1 = x + ga1 * _dot(mix, wout_ref[...])
        ms = jnp.mean(x1 * x1, axis=-1, keepdims=True)
        return x1, ((x1 * lax.rsqrt(ms + EPS)) * coef2 + sh2).astype(BF16)

    subs = [slice(r0, r0 + POST_SUB) for r0 in range(0, TM, POST_SUB)]
    ys = []
    nxt = head(subs[0])
    for s, rows in enumerate(subs):
        x1, h2 = nxt
        a = jnp.maximum(_dot(h2, wff1_ref[...]), 0.0)
        if s + 1 < len(subs):
            nxt = head(subs[s + 1])
        x2 = x1 + ga2 * _dot((a * a).astype(BF16), wff2_ref[...])
        if final:
            ms2 = jnp.mean(x2 * x2, axis=-1, keepdims=True)
            x2 = (x2 * lax.rsqrt(ms2 + EPS)) * nfin_ref[...]
        ys.append((rows, x2))

    if not final:
        for rows, y in ys:
            out_refs[0][rows, :] = y
        return
    yc_ref, yl_ref = out_refs

    @pl.when(is_ctx)
    def _():
        for rows, y in ys:
            yc_ref[rows, :] = y

    @pl.when(jnp.logical_not(is_ctx))
    def _():
        for rows, y in ys:
            yl_ref[rows, :] = y


def _post_call(layer, final, xs, o_fwd, o_bwd, rest, mod, o_norm, ws, bs_t, w_out, norm_ffn, w_ff1,
               w_ff2, norm_final):
    split = len(xs) == 2
    const = dict(pipeline_mode=pl.Buffered(1))
    if split:
        x_specs = [pl.BlockSpec((TM, D_MODEL), lambda i: (jnp.minimum(i, CTX_TILES - 1), 0)),
                   pl.BlockSpec((TM, D_MODEL), lambda i: (jnp.maximum(i - CTX_TILES, 0), 0))]
    else:
        x_specs = [pl.BlockSpec((TM, D_MODEL), lambda i: (i, 0))]
    if final:
        out_specs = [
            pl.BlockSpec((TM, D_MODEL), lambda i: (jnp.minimum(i, CTX_TILES - 1), 0)),
            pl.BlockSpec((TM, D_MODEL), lambda i: (jnp.maximum(i - CTX_TILES, 0), 0)),
        ]
        out_shape = [jax.ShapeDtypeStruct((N_CTX, D_MODEL), F32),
                     jax.ShapeDtypeStruct((N_LAT, D_MODEL), F32)]
    else:
        out_specs = pl.BlockSpec((TM, D_MODEL), lambda i: (i, 0))
        out_shape = jax.ShapeDtypeStruct((N_TOK, D_MODEL), F32)
    return pl.pallas_call(
        functools.partial(_post_kernel, final, split),
        grid=(N_TILES,),
        in_specs=x_specs + [
            pl.BlockSpec((TM, D_A), lambda i: (i, 0)),
            pl.BlockSpec((TM, D_A), lambda i: (i, 0)),
            pl.BlockSpec((TM, 3 * D_A), lambda i: (i, 0)),
            pl.BlockSpec((None, 1, 1, N_MOD * D_MODEL), lambda i: (layer, _mod_row(i), 0, 0)),
            pl.BlockSpec((None, 1, DV), lambda i: (layer, 0, 0), **const),
            pl.BlockSpec((None, G_B, CHUNK_MLP, CHUNK_MLP), lambda i: (layer, 0, 0, 0), **const),
            pl.BlockSpec((None, CHUNK_MLP, LANES), lambda i: (layer, 0, 0), **const),
            pl.BlockSpec((D_MODEL, D_MODEL), lambda i: (0, 0), **const),
            pl.BlockSpec((None, 1, D_MODEL), lambda i: (layer, 0, 0), **const),
            pl.BlockSpec((D_MODEL, D_FF), lambda i: (0, 0), **const),
            pl.BlockSpec((D_FF, D_MODEL), lambda i: (0, 0), **const),
            pl.BlockSpec((1, D_MODEL), lambda i: (0, 0), **const),
        ],
        out_specs=out_specs,
        out_shape=out_shape,
        compiler_params=pltpu.CompilerParams(
            dimension_semantics=("arbitrary",), vmem_limit_bytes=VMEM_LIMIT_BYTES),
        name=f"post_l{layer}",
    )(*xs, o_fwd, o_bwd, rest, mod, o_norm, ws, bs_t, w_out, norm_ffn, w_ff1, w_ff2, norm_final)


def kernel(x_prompt, x_sample, state_delta, c, c_ctx, w_mod, b_mod, norm_mix, w_in, conv_qkv, a_log,
           dt_bias, o_norm, w_spatial, b_spatial, w_out, norm_ffn, w_ff1, w_ff2, norm_final):
    cond8 = jnp.concatenate([c_ctx[None, :], c, jnp.zeros((8 - 1 - LAT_B, D_MODEL), F32)], axis=0)

    n_qkvg = 4 * D_A
    w_main = jnp.concatenate([w_in[:, :, :n_qkvg], w_in[:, :, n_qkvg + 4 * H_A:]], axis=-1).astype(BF16)
    w_ab = jnp.pad(w_in[:, :, n_qkvg:n_qkvg + 4 * H_A], ((0, 0), (0, 0), (0, LANES - 4 * H_A))).astype(BF16)
    a_row = jnp.pad(a_log.reshape(DEPTH, 1, 2 * H_A), ((0, 0), (0, 0), (2 * H_A, LANES - 4 * H_A)))
    dt_row = jnp.pad(dt_bias.reshape(DEPTH, 1, 2 * H_A), ((0, 0), (0, 0), (2 * H_A, LANES - 4 * H_A)))
    bs_t = jnp.pad(jnp.swapaxes(b_spatial, 1, 2), ((0, 0), (0, 0), (0, LANES - G_B)))
    ws_bf = w_spatial.astype(BF16)
    norm_mix3 = norm_mix.reshape(DEPTH, 1, D_MODEL)
    norm_ffn3 = norm_ffn.reshape(DEPTH, 1, D_MODEL)
    o_norm3 = o_norm.reshape(DEPTH, 1, DV)
    norm_final2 = norm_final.reshape(1, D_MODEL)

    mod = _modulation(cond8, w_mod, b_mod).reshape(DEPTH, 8, 1, N_MOD * D_MODEL)

    xs = (x_prompt.reshape(N_CTX, D_MODEL), x_sample.reshape(N_LAT, D_MODEL))
    new_state = jnp.zeros((CTX_B, DEPTH, 2, H_A, DK, DV), F32)
    for layer in range(DEPTH):
        qkv, rest, gb = _pre_call(layer, xs, mod, norm_mix3, w_main, w_ab, conv_qkv, a_row, dt_row)
        o_fwd, o_bwd, new_state, w_out_bf, w_ff1_bf, w_ff2_bf = _delta_call(
            layer, qkv, gb, state_delta, new_state, w_out, w_ff1, w_ff2)
        out = _post_call(layer, layer == DEPTH - 1, xs, o_fwd, o_bwd, rest, mod, o_norm3, ws_bf, bs_t,
                         w_out_bf, norm_ffn3, w_ff1_bf, w_ff2_bf, norm_final2)
        xs = (out,)

    y_prompt, y_sample = out
    return (y_prompt.reshape(CTX_B, CTX_T, D_MODEL), y_sample.reshape(LAT_B, LAT_T, D_MODEL), new_state)
```

```python
import functools

import jax
import jax.numpy as jnp
from jax import lax
from jax.experimental import pallas as pl
from jax.experimental.pallas import tpu as pltpu

F32 = jnp.float32
BF16 = jnp.bfloat16

D_MODEL = 1024
DEPTH = 4
N_MOD = 6
H_A = 4
DK = 128
DV = 128
D_A = H_A * DK
G_B = 4
C_B = 128
D_B = G_B * C_B
CHUNK_MLP = 128
D_FF = 4 * D_MODEL
EPS = 1e-6

CTX_B, CTX_T = 32, 256
LAT_B, LAT_T = 2, 4096
N_CTX = CTX_B * CTX_T
N_LAT = LAT_B * LAT_T
N_TOK = N_CTX + N_LAT

LANES = 128
MOD_TN = 3072
TM = 512
N_TILES = N_TOK // TM
CTX_TILES = N_CTX // TM
LAT_TILES_PER_SEQ = LAT_T // TM
HALO = 8
POST_SUB = 256

CHUNK = 128
INV_BASE = 8
BF16_TILE_ROWS = 16
TD = 512
CHUNKS_PER_TD = TD // CHUNK
SEQ_PER_CTX_DTILE = TD // CTX_T
CTX_DTILES = N_CTX // TD
LAT_DTILES_PER_SEQ = LAT_T // TD
N_DTILES = N_TOK // TD

N_FEAT = 6 * D_A
VMEM_LIMIT_BYTES = 60000 * 1024


def _sigmoid(x):
    return 1.0 / (1.0 + jnp.exp(-x))


def _silu(x):
    hx = 0.5 * x
    return hx + hx * jnp.tanh(hx)


_GELU_C0 = 0.7978845608028654
_GELU_C1 = _GELU_C0 * 0.044715


def _gelu_tanh(x):
    hx = 0.5 * x
    return hx + hx * jnp.tanh(x * (_GELU_C0 + _GELU_C1 * (x * x)))


def _softplus(x):
    return jnp.maximum(x, 0.0) + jnp.log1p(jnp.exp(-jnp.abs(x)))


def _dot(a, b):
    return jnp.dot(a, b, preferred_element_type=F32)


def _dot_nt(a, b):
    return lax.dot_general(a, b, (((1,), (1,)), ((), ())), preferred_element_type=F32)


def _dot_tn(a, b):
    return lax.dot_general(a, b, (((0,), (0,)), ((), ())), preferred_element_type=F32)


def _mod_kernel(cond_ref, w_ref, b_ref, o_ref):
    c = cond_ref[...]
    s = (c * _sigmoid(c)).astype(BF16)
    o_ref[0] = _dot(s, w_ref[0].astype(BF16)) + b_ref[0]


def _modulation(cond8, w_mod, b_mod):
    tn = MOD_TN
    return pl.pallas_call(
        _mod_kernel,
        grid=(DEPTH, (N_MOD * D_MODEL) // tn),
        in_specs=[
            pl.BlockSpec((8, D_MODEL), lambda l, j: (0, 0)),
            pl.BlockSpec((1, D_MODEL, tn), lambda l, j: (l, 0, j)),
            pl.BlockSpec((1, 1, tn), lambda l, j: (l, 0, j)),
        ],
        out_specs=pl.BlockSpec((1, 8, tn), lambda l, j: (l, 0, j)),
        out_shape=jax.ShapeDtypeStruct((DEPTH, 8, N_MOD * D_MODEL), F32),
        compiler_params=pltpu.CompilerParams(
            dimension_semantics=("arbitrary", "arbitrary"), vmem_limit_bytes=VMEM_LIMIT_BYTES),
        name="modulation",
    )(cond8, w_mod, b_mod.reshape(DEPTH, 1, N_MOD * D_MODEL))


def _mod_row(i):
    return jnp.where(i < CTX_TILES, 0, 1 + (i - CTX_TILES) // LAT_TILES_PER_SEQ)


def _pre_kernel(split, *refs):
    i = pl.program_id(0)
    is_ctx = i < CTX_TILES
    if split:
        (xc_ref, xl_ref, xcp_ref, xlp_ref, xcn_ref, xln_ref, mod_ref, nw_ref, w_ref, wab_ref,
         conv_ref, arow_ref, dtrow_ref, qkv_ref, rest_ref, gb_ref) = refs
        x = jnp.where(is_ctx, xc_ref[...], xl_ref[...])
        x_prev = jnp.where(is_ctx, xcp_ref[...], xlp_ref[...])
        x_next = jnp.where(is_ctx, xcn_ref[...], xln_ref[...])
    else:
        (x_ref, xp_ref, xn_ref, mod_ref, nw_ref, w_ref, wab_ref,
         conv_ref, arow_ref, dtrow_ref, qkv_ref, rest_ref, gb_ref) = refs
        x, x_prev, x_next = x_ref[...], xp_ref[...], xn_ref[...]
    m = mod_ref[0]
    sh1 = m[:, 0:D_MODEL]
    coef = nw_ref[...] * (1.0 + m[:, D_MODEL:2 * D_MODEL])

    def norm_mod(xx):
        ms = jnp.mean(xx * xx, axis=-1, keepdims=True)
        return ((xx * lax.rsqrt(ms + EPS)) * coef + sh1).astype(BF16)

    h = norm_mod(x)
    hh = norm_mod(jnp.concatenate([x_prev, x_next], axis=0))
    h_ext = jnp.concatenate([h, hh], axis=0)

    lat_j = (i - CTX_TILES) % LAT_TILES_PER_SEQ
    starts_seq = jnp.logical_or(is_ctx, lat_j == 0)
    ends_seq = jnp.logical_or(is_ctx, lat_j == LAT_TILES_PER_SEQ - 1)
    r8 = lax.broadcasted_iota(jnp.int32, (HALO, 1), 0)

    def shifted(p, prev_row, next_row):
        dn = pltpu.roll(p, 1, 0)
        up = pltpu.roll(p, TM - 1, 0)
        dn_parts, up_parts = [], []
        for b in range(0, TM, CTX_T):
            slab = dn[b:b + HALO]
            if b == 0:
                slab = jnp.where(r8 == 0, jnp.where(starts_seq, 0.0, prev_row), slab)
            else:
                slab = jnp.where(jnp.logical_and(r8 == 0, is_ctx), 0.0, slab)
            dn_parts += [slab, dn[b + HALO:b + CTX_T]]
            e = b + CTX_T
            slab = up[e - HALO:e]
            if e == TM:
                slab = jnp.where(r8 == HALO - 1, jnp.where(ends_seq, 0.0, next_row), slab)
            else:
                slab = jnp.where(jnp.logical_and(r8 == HALO - 1, is_ctx), 0.0, slab)
            up_parts += [up[b:e - HALO], slab]
        return jnp.concatenate(dn_parts, axis=0), jnp.concatenate(up_parts, axis=0)

    cw = conv_ref[...]
    for blk in range(3):
        cols = slice(blk * D_A, (blk + 1) * D_A)
        pe = _dot(h_ext, w_ref[:, cols])
        rcols = slice((3 + blk) * D_A, (4 + blk) * D_A)
        rest_ref[:, blk * D_A:(blk + 1) * D_A] = _dot(h, w_ref[:, rcols]).astype(BF16)
        if blk == 0:
            ab = _dot(h, wab_ref[...])
            lane = lax.broadcasted_iota(jnp.int32, (TM, LANES), 1)
            g_log = -jnp.exp(arow_ref[...]) * _softplus(ab + dtrow_ref[...])
            gb_ref[...] = jnp.where(lane < 2 * H_A, _sigmoid(ab), g_log)
        p = pe[0:TM]
        xm1, xp1 = shifted(p, pe[TM + HALO - 1:TM + HALO], pe[TM + HALO:TM + HALO + 1])
        y = _silu(cw[0:1, cols] * xm1 + cw[1:2, cols] * p + cw[2:3, cols] * xp1)
        if blk < 2:
            scale = DK ** -0.5 if blk == 0 else 1.0
            for hd in range(H_A):
                yh = y[:, hd * DK:(hd + 1) * DK]
                inv_norm = lax.rsqrt(jnp.sum(yh * yh, axis=-1, keepdims=True) + EPS) * scale
                qkv_ref[:, blk * D_A + hd * DK:blk * D_A + (hd + 1) * DK] = (yh * inv_norm).astype(BF16)
        else:
            qkv_ref[:, cols] = y.astype(BF16)


def _pre_call(layer, xs, mod, norm_mix, w_main, w_ab, conv_qkv, a_row, dt_row):
    split = len(xs) == 2
    bpt = TM // HALO
    const = dict(pipeline_mode=pl.Buffered(1))

    def tile_specs(first_tile, n_tiles):
        def local(i):
            return jnp.clip(i - first_tile, 0, n_tiles - 1)
        return (
            pl.BlockSpec((TM, D_MODEL), lambda i: (local(i), 0)),
            pl.BlockSpec((HALO, D_MODEL), lambda i: (jnp.maximum(local(i) * bpt - 1, 0), 0)),
            pl.BlockSpec((HALO, D_MODEL),
                         lambda i: (jnp.minimum((local(i) + 1) * bpt, n_tiles * bpt - 1), 0)),
        )

    if split:
        c_specs = tile_specs(0, CTX_TILES)
        l_specs = tile_specs(CTX_TILES, N_TILES - CTX_TILES)
        x_specs = [c_specs[0], l_specs[0], c_specs[1], l_specs[1], c_specs[2], l_specs[2]]
        x_args = [xs[0], xs[1]] * 3
    else:
        x_specs = list(tile_specs(0, N_TILES))
        x_args = [xs[0]] * 3
    return pl.pallas_call(
        functools.partial(_pre_kernel, split),
        grid=(N_TILES,),
        in_specs=x_specs + [
            pl.BlockSpec((None, 1, 1, N_MOD * D_MODEL), lambda i: (layer, _mod_row(i), 0, 0)),
            pl.BlockSpec((None, 1, D_MODEL), lambda i: (layer, 0, 0), **const),
            pl.BlockSpec((D_MODEL, N_FEAT), lambda i: (0, 0), **const),
            pl.BlockSpec((D_MODEL, LANES), lambda i: (0, 0), **const),
            pl.BlockSpec((None, 3, 3 * D_A), lambda i: (layer, 0, 0), **const),
            pl.BlockSpec((None, 1, LANES), lambda i: (layer, 0, 0), **const),
            pl.BlockSpec((None, 1, LANES), lambda i: (layer, 0, 0), **const),
        ],
        out_specs=[
            pl.BlockSpec((TM, 3 * D_A), lambda i: (i, 0)),
            pl.BlockSpec((TM, 3 * D_A), lambda i: (i, 0)),
            pl.BlockSpec((TM, LANES), lambda i: (i, 0)),
        ],
        out_shape=[
            jax.ShapeDtypeStruct((N_TOK, 3 * D_A), BF16),
            jax.ShapeDtypeStruct((N_TOK, 3 * D_A), BF16),
            jax.ShapeDtypeStruct((N_TOK, LANES), F32),
        ],
        compiler_params=pltpu.CompilerParams(
            dimension_semantics=("arbitrary",), vmem_limit_bytes=VMEM_LIMIT_BYTES),
        name=f"pre_l{layer}",
    )(*x_args, mod, norm_mix, w_main, w_ab, conv_qkv, a_row, dt_row)


def _delta_tile_row(rev, t):
    if not rev:
        return t
    lat = t - CTX_DTILES
    b = lat // LAT_DTILES_PER_SEQ
    j = LAT_DTILES_PER_SEQ - 1 - lat % LAT_DTILES_PER_SEQ
    return jnp.where(t < CTX_DTILES, t, CTX_DTILES + b * LAT_DTILES_PER_SEQ + j)


def _mm_inv(a, b):
    return _dot(a.astype(BF16), b.astype(BF16))


def _row_groups(x, size):
    return [x[g * size:(g + 1) * size] for g in range(x.shape[0] // size)]


def _delta_direction(rev, is_ctx, q_ref, k_ref, v_ref, gb_ref, o_ref, state0):
    row = lax.broadcasted_iota(jnp.int32, (CHUNK, CHUNK), 0)
    col = lax.broadcasted_iota(jnp.int32, (CHUNK, CHUNK), 1)
    incl = row <= col if rev else row >= col
    strict = row < col if rev else row > col
    eye = (row == col).astype(F32)
    base_mask = (row // INV_BASE) == (col // INV_BASE)
    levels = []
    blk = INV_BASE
    while blk < CHUNK:
        levels.append((blk, jnp.logical_and((row // (2 * blk)) == (col // (2 * blk)),
                                            (row // blk) != (col // blk))))
        blk *= 2
    rix = lax.broadcasted_iota(jnp.int32, (CHUNK, LANES), 0)
    lane_beta = H_A if rev else 0
    lane_g = 3 * H_A if rev else 2 * H_A
    last = 0 if rev else CHUNK - 1

    chunk_order = range(CHUNKS_PER_TD - 1, -1, -1) if rev else range(CHUNKS_PER_TD)
    chunk_rows = [slice(cc * CHUNK, (cc + 1) * CHUNK) for cc in chunk_order]
    probs = [[dict(hd=hd, hcols=slice(hd * DK, (hd + 1) * DK)) for hd in range(H_A)]
             for _ in range(CHUNKS_PER_TD)]
    carry = dict(state=list(state0), mid=None)

    def prep(c):
        rows = chunk_rows[c]
        gb = gb_ref[rows, :]
        fwd_sum = gb
        s = 1
        while s < CHUNK:
            fwd_sum = fwd_sum + jnp.where(rix >= s, pltpu.roll(fwd_sum, s, 0), 0.0)
            s *= 2
        cum = fwd_sum[CHUNK - 1:CHUNK, :] - fwd_sum + gb if rev else fwd_sum
        cum_t = cum.T
        for p in probs[c]:
            hd = p["hd"]
            g_b = jnp.broadcast_to(cum[:, lane_g + hd:lane_g + hd + 1], (CHUNK, CHUNK))
            p.update(beta_b=jnp.broadcast_to(gb[:, lane_beta + hd:lane_beta + hd + 1], (CHUNK, CHUNK)),
                     g_b=g_b,
                     g_row=cum_t[lane_g + hd:lane_g + hd + 1, :],
                     g_tot=g_b[last:last + 1, :])

    def kq_stage(c):
        rows = chunk_rows[c]
        for p in probs[c]:
            k_bf = k_ref[rows, p["hcols"]]
            kb = (k_bf.astype(F32) * p["beta_b"]).astype(BF16)
            kq = _dot_nt(jnp.concatenate([kb, q_ref[rows, p["hcols"]]], axis=0), k_bf)
            decay = jnp.exp(jnp.minimum(p["g_b"] - p["g_row"], 0.0))
            p["a"] = jnp.where(strict, kq[0:CHUNK] * decay, 0.0)
            p["qk"] = jnp.where(incl, kq[CHUNK:2 * CHUNK] * decay, 0.0).astype(BF16)

    def base_first(c):
        for p in probs[c]:
            a0 = jnp.where(base_mask, p["a"], 0.0)
            p["inv"] = eye - a0
            p["pw"] = _mm_inv(a0, a0).astype(BF16)

    def base_mid(c):
        for p in probs[c]:
            st = _dot(jnp.concatenate([p["pw"], p["inv"].astype(BF16)], axis=0), p["pw"])
            p["inv"] = p["inv"] + st[CHUNK:2 * CHUNK]
            p["pw"] = st[0:CHUNK].astype(BF16)

    def base_last(c):
        for p in probs[c]:
            p["inv"] = p["inv"] + _dot(p["inv"].astype(BF16), p["pw"])

    upd_parity = 0 if rev else 1

    def merge_first(size, off_mask, c):
        for p in probs[c]:
            p["groups"] = _row_groups(p["inv"], size)
            lhs = jnp.concatenate(p["groups"][upd_parity::2], axis=0)
            p["tmp"] = _mm_inv(lhs, jnp.where(off_mask, p["a"], 0.0))

    def merge_second(size, c):
        for p in probs[c]:
            upd = _row_groups(_mm_inv(p["tmp"], p["inv"]), size)
            new = [g - upd[i // 2] if i % 2 == upd_parity else g for i, g in enumerate(p["groups"])]
            p["inv"] = jnp.concatenate(new, axis=0)

    def to_bf16(c):
        for p in probs[c]:
            p["inv"] = p["inv"].astype(BF16)
            p["a"] = p["a"].astype(BF16)

    def merge_first_bf16(size, mask_bf, c):
        for p in probs[c]:
            p["groups"] = _row_groups(p["inv"], size)
            lhs = jnp.concatenate(p["groups"][upd_parity::2], axis=0)
            p["tmp"] = _dot(lhs, p["a"] * mask_bf).astype(BF16)

    def merge_second_bf16(size, c):
        for p in probs[c]:
            upd = _row_groups(_dot(p["tmp"], p["inv"]), size)
            new = [(g.astype(F32) - upd[i // 2]).astype(BF16) if i % 2 == upd_parity else g
                   for i, g in enumerate(p["groups"])]
            p["inv"] = jnp.concatenate(new, axis=0)

    def solve(c):
        rows = chunk_rows[c]
        for p in probs[c]:
            beta_b = p["beta_b"]
            rhs = jnp.concatenate(
                [v_ref[rows, p["hcols"]].astype(F32) * beta_b,
                 k_ref[rows, p["hcols"]].astype(F32) * beta_b * jnp.exp(p["g_b"])], axis=-1)
            sol = _dot(p["inv"].astype(BF16), rhs.astype(BF16))
            p["u"] = sol[:, 0:DV]
            p["w"] = sol[:, DV:2 * DV].astype(BF16)

    def rec_first(c):
        if c == CHUNKS_PER_TD // 2:
            carry["mid"] = carry["state"]
            carry["state"] = [jnp.where(is_ctx, 0.0, st) for st in carry["state"]]
        for p in probs[c]:
            p["st_bf"] = carry["state"][p["hd"]].astype(BF16)
            p["ws"] = _dot(p["w"], p["st_bf"])

    def rec_second(c):
        state = list(carry["state"])
        rows = chunk_rows[c]
        for p in probs[c]:
            g_b, g_tot = p["g_b"], p["g_tot"]
            q_dec = (q_ref[rows, p["hcols"]].astype(F32) * jnp.exp(g_b)).astype(BF16)
            k_dec = (k_ref[rows, p["hcols"]].astype(F32) * jnp.exp(g_tot - g_b)).astype(BF16)
            v_new = (p["u"] - p["ws"]).astype(BF16)
            o_ref[rows, p["hcols"]] = _dot(jnp.concatenate([q_dec, p["qk"]], axis=1),
                                           jnp.concatenate([p["st_bf"], v_new], axis=0))
            state[p["hd"]] = state[p["hd"]] * jnp.exp(g_tot) + _dot_tn(k_dec, v_new)
        carry["state"] = state

    stages = [prep, kq_stage, base_first]
    n = 2
    while 2 * n < INV_BASE:
        stages.append(base_mid)
        n *= 2
    stages.append(base_last)
    in_bf16 = False
    for size, off_mask in levels:
        if size % BF16_TILE_ROWS == 0 and not in_bf16:
            stages.append(to_bf16)
            in_bf16 = True
        if in_bf16:
            mask_bf = jnp.where(off_mask, 1.0, 0.0).astype(BF16)
            stages.append(functools.partial(merge_first_bf16, size, mask_bf))
            stages.append(functools.partial(merge_second_bf16, size))
        else:
            stages.append(functools.partial(merge_first, size, off_mask))
            stages.append(functools.partial(merge_second, size))
    stages.append(solve)
    return stages, rec_first, rec_second, carry


def _delta_kernel(qf_ref, kf_ref, vf_ref, gbf_ref, qb_ref, kb_ref, vb_ref, gbb_ref, s0_ref,
                  wout_ref, wff1_ref, wff2_ref, win_ref, ns_in_ref,
                  of_ref, ob_ref, ns_ref, wout_bf_ref, wff1_bf_ref, wff2_bf_ref, wmain_bf_ref,
                  wab_bf_ref, s_scr):
    del ns_in_ref
    wout_bf_ref[...] = wout_ref[...].astype(BF16)
    wff1_bf_ref[...] = wff1_ref[...].astype(BF16)
    wff2_bf_ref[...] = wff2_ref[...].astype(BF16)
    w_in_rows = win_ref[...]
    n_ab = 4 * H_A
    wmain_bf_ref[...] = jnp.concatenate(
        [w_in_rows[:, 0:4 * D_A], w_in_rows[:, 4 * D_A + n_ab:]], axis=-1).astype(BF16)
    wab_bf_ref[...] = jnp.concatenate(
        [w_in_rows[:, 4 * D_A:4 * D_A + n_ab],
         jnp.zeros((w_in_rows.shape[0], LANES - n_ab), F32)], axis=-1).astype(BF16)
    t = pl.program_id(0)
    is_ctx = t < CTX_DTILES

    @pl.when(is_ctx)
    def _():
        s_scr[...] = jnp.zeros_like(s_scr)

    @pl.when(jnp.logical_and(t >= CTX_DTILES, (t - CTX_DTILES) % LAT_DTILES_PER_SEQ == 0))
    def _():
        s_scr[...] = s0_ref[...]

    dirs = [
        _delta_direction(False, is_ctx, qf_ref, kf_ref, vf_ref, gbf_ref, of_ref,
                         [s_scr[0, hd] for hd in range(H_A)]),
        _delta_direction(True, is_ctx, qb_ref, kb_ref, vb_ref, gbb_ref, ob_ref,
                         [s_scr[1, hd] for hd in range(H_A)]),
    ]
    for k in range(len(dirs[0][0])):
        for stages, _, _, _ in dirs:
            for c in range(CHUNKS_PER_TD):
                stages[k](c)
    for c in range(CHUNKS_PER_TD):
        for _, rec_first, _, _ in dirs:
            rec_first(c)
        for _, _, rec_second, _ in dirs:
            rec_second(c)

    for d, (_, _, _, carry) in enumerate(dirs):
        for hd in range(H_A):
            s_scr[d, hd] = carry["state"][hd]

    @pl.when(is_ctx)
    def _():
        for d, (_, _, _, carry) in enumerate(dirs):
            first, second = (1, 0) if d == 1 else (0, 1)
            for hd in range(H_A):
                ns_ref[first, d, hd] = carry["mid"][hd]
                ns_ref[second, d, hd] = carry["state"][hd]


def _delta_call(layer, qkv, gb, state_delta, new_state, w_out, w_ff1, w_ff2, w_in):
    wo_rows, f1_cols, f2_rows = D_MODEL // N_DTILES, D_FF // N_DTILES, D_FF // N_DTILES
    nxt = min(layer + 1, DEPTH - 1)
    proj = w_in.shape[-1]

    def tile_specs(rev):
        def spec(width, col_block):
            return pl.BlockSpec((TD, width), lambda t: (_delta_tile_row(rev, t), col_block))
        return [spec(D_A, 0), spec(D_A, 1), spec(D_A, 2), spec(LANES, 0)]

    def s0_index(t):
        b = jnp.maximum(t - CTX_DTILES, 0) // LAT_DTILES_PER_SEQ
        return (b, layer, 0, 0, 0, 0)

    def ns_index(t):
        return (jnp.minimum(t, CTX_DTILES - 1), layer, 0, 0, 0, 0)

    return pl.pallas_call(
        _delta_kernel,
        grid=(N_DTILES,),
        in_specs=tile_specs(False) + tile_specs(True) + [
            pl.BlockSpec((None, None, 2, H_A, DK, DV), s0_index),
            pl.BlockSpec((None, wo_rows, D_MODEL), lambda t: (layer, t, 0)),
            pl.BlockSpec((None, D_MODEL, f1_cols), lambda t: (layer, 0, t)),
            pl.BlockSpec((None, f2_rows, D_MODEL), lambda t: (layer, t, 0)),
            pl.BlockSpec((None, wo_rows, proj), lambda t: (nxt, t, 0)),
            pl.BlockSpec(memory_space=pl.ANY),
        ],
        out_specs=[
            pl.BlockSpec((TD, D_A), lambda t: (_delta_tile_row(False, t), 0)),
            pl.BlockSpec((TD, D_A), lambda t: (_delta_tile_row(True, t), 0)),
            pl.BlockSpec((SEQ_PER_CTX_DTILE, None, 2, H_A, DK, DV), ns_index),
            pl.BlockSpec((wo_rows, D_MODEL), lambda t: (t, 0)),
            pl.BlockSpec((D_MODEL, f1_cols), lambda t: (0, t)),
            pl.BlockSpec((f2_rows, D_MODEL), lambda t: (t, 0)),
            pl.BlockSpec((wo_rows, N_FEAT), lambda t: (t, 0)),
            pl.BlockSpec((wo_rows, LANES), lambda t: (t, 0)),
        ],
        out_shape=[
            jax.ShapeDtypeStruct((N_TOK, D_A), F32),
            jax.ShapeDtypeStruct((N_TOK, D_A), F32),
            jax.ShapeDtypeStruct(new_state.shape, F32),
            jax.ShapeDtypeStruct((D_MODEL, D_MODEL), BF16),
            jax.ShapeDtypeStruct((D_MODEL, D_FF), BF16),
            jax.ShapeDtypeStruct((D_FF, D_MODEL), BF16),
            jax.ShapeDtypeStruct((D_MODEL, N_FEAT), BF16),
            jax.ShapeDtypeStruct((D_MODEL, LANES), BF16),
        ],
        scratch_shapes=[pltpu.VMEM((2, H_A, DK, DV), F32)],
        input_output_aliases={13: 2},
        compiler_params=pltpu.CompilerParams(
            dimension_semantics=("arbitrary",), vmem_limit_bytes=VMEM_LIMIT_BYTES),
        name=f"delta_l{layer}",
    )(qkv, qkv, qkv, gb, qkv, qkv, qkv, gb, state_delta, w_out, w_ff1, w_ff2, w_in, new_state)


def _post_kernel(final, split, *refs):
    is_ctx = pl.program_id(0) < CTX_TILES
    (of_ref, ob_ref, rest_ref, mod_ref, onorm_ref, ws_ref, bs_ref, wout_ref, nffn_ref, wff1_ref,
     wff2_ref, nfin_ref) = refs[1 + split:13 + split]
    out_refs = refs[13 + split:]
    m = mod_ref[0]
    ga1 = m[:, 2 * D_MODEL:3 * D_MODEL]
    sh2 = m[:, 3 * D_MODEL:4 * D_MODEL]
    coef2 = nffn_ref[...] * (1.0 + m[:, 4 * D_MODEL:5 * D_MODEL])
    ga2 = m[:, 5 * D_MODEL:6 * D_MODEL]
    onw = onorm_ref[...]
    bs = bs_ref[...]

    def head(rows):
        if split:
            x = jnp.where(is_ctx, refs[0][rows, :], refs[1][rows, :])
        else:
            x = refs[0][rows, :]
        o = of_ref[rows, :] + ob_ref[rows, :]
        mix_in = []
        for hd in range(H_A):
            oh = o[:, hd * DV:(hd + 1) * DV]
            oh = oh * (lax.rsqrt(jnp.mean(oh * oh, axis=-1, keepdims=True) + EPS) * onw)
            gate = _silu(rest_ref[rows, hd * DV:(hd + 1) * DV].astype(F32))
            mix_in.append((oh * gate).astype(BF16))
        for g in range(G_B):
            ug = _gelu_tanh(rest_ref[rows, D_A + g * C_B:D_A + (g + 1) * C_B].astype(F32))
            vg = _gelu_tanh(rest_ref[rows, 2 * D_A + g * C_B:2 * D_A + (g + 1) * C_B].astype(F32))
            dv = vg - jnp.mean(vg, axis=-1, keepdims=True)
            var = jnp.mean(dv * dv, axis=-1, keepdims=True)
            vgn = (dv * lax.rsqrt(var + EPS)).astype(BF16)
            parts = []
            for c in range(POST_SUB // CHUNK_MLP):
                crows = slice(c * CHUNK_MLP, (c + 1) * CHUNK_MLP)
                mixed = _dot(ws_ref[g], vgn[crows]) + bs[:, g:g + 1]
                parts.append((ug[crows] * mixed).astype(BF16))
            mix_in.append(jnp.concatenate(parts, axis=0))
        mix = jnp.concatenate(mix_in, axis=-1)
        x1 = x + ga1 * _dot(mix, wout_ref[...])
        ms = jnp.mean(x1 * x1, axis=-1, keepdims=True)
        return x1, ((x1 * lax.rsqrt(ms + EPS)) * coef2 + sh2).astype(BF16)

    subs = [slice(r0, r0 + POST_SUB) for r0 in range(0, TM, POST_SUB)]
    ys = []
    nxt = head(subs[0])
    for s, rows in enumerate(subs):
        x1, h2 = nxt
        a = jnp.maximum(_dot(h2, wff1_ref[...]), 0.0)
        if s + 1 < len(subs):
            nxt = head(subs[s + 1])
        x2 = x1 + ga2 * _dot((a * a).astype(BF16), wff2_ref[...])
        if final:
            ms2 = jnp.mean(x2 * x2, axis=-1, keepdims=True)
            x2 = (x2 * lax.rsqrt(ms2 + EPS)) * nfin_ref[...]
        ys.append((rows, x2))

    if not final:
        for rows, y in ys:
            out_refs[0][rows, :] = y
        return
    yc_ref, yl_ref = out_refs

    @pl.when(is_ctx)
    def _():
        for rows, y in ys:
            yc_ref[rows, :] = y

    @pl.when(jnp.logical_not(is_ctx))
    def _():
        for rows, y in ys:
            yl_ref[rows, :] = y


def _post_call(layer, final, xs, o_fwd, o_bwd, rest, mod, o_norm, ws, bs_t, w_out, norm_ffn, w_ff1,
               w_ff2, norm_final):
    split = len(xs) == 2
    const = dict(pipeline_mode=pl.Buffered(1))
    if split:
        x_specs = [pl.BlockSpec((TM, D_MODEL), lambda i: (jnp.minimum(i, CTX_TILES - 1), 0)),
                   pl.BlockSpec((TM, D_MODEL), lambda i: (jnp.maximum(i - CTX_TILES, 0), 0))]
    else:
        x_specs = [pl.BlockSpec((TM, D_MODEL), lambda i: (i, 0))]
    if final:
        out_specs = [
            pl.BlockSpec((TM, D_MODEL), lambda i: (jnp.minimum(i, CTX_TILES - 1), 0)),
            pl.BlockSpec((TM, D_MODEL), lambda i: (jnp.maximum(i - CTX_TILES, 0), 0)),
        ]
        out_shape = [jax.ShapeDtypeStruct((N_CTX, D_MODEL), F32),
                     jax.ShapeDtypeStruct((N_LAT, D_MODEL), F32)]
    else:
        out_specs = pl.BlockSpec((TM, D_MODEL), lambda i: (i, 0))
        out_shape = jax.ShapeDtypeStruct((N_TOK, D_MODEL), F32)
    return pl.pallas_call(
        functools.partial(_post_kernel, final, split),
        grid=(N_TILES,),
        in_specs=x_specs + [
            pl.BlockSpec((TM, D_A), lambda i: (i, 0)),
            pl.BlockSpec((TM, D_A), lambda i: (i, 0)),
            pl.BlockSpec((TM, 3 * D_A), lambda i: (i, 0)),
            pl.BlockSpec((None, 1, 1, N_MOD * D_MODEL), lambda i: (layer, _mod_row(i), 0, 0)),
            pl.BlockSpec((None, 1, DV), lambda i: (layer, 0, 0), **const),
            pl.BlockSpec((None, G_B, CHUNK_MLP, CHUNK_MLP), lambda i: (layer, 0, 0, 0), **const),
            pl.BlockSpec((None, CHUNK_MLP, LANES), lambda i: (layer, 0, 0), **const),
            pl.BlockSpec((D_MODEL, D_MODEL), lambda i: (0, 0), **const),
            pl.BlockSpec((None, 1, D_MODEL), lambda i: (layer, 0, 0), **const),
            pl.BlockSpec((D_MODEL, D_FF), lambda i: (0, 0), **const),
            pl.BlockSpec((D_FF, D_MODEL), lambda i: (0, 0), **const),
            pl.BlockSpec((1, D_MODEL), lambda i: (0, 0), **const),
        ],
        out_specs=out_specs,
        out_shape=out_shape,
        compiler_params=pltpu.CompilerParams(
            dimension_semantics=("arbitrary",), vmem_limit_bytes=VMEM_LIMIT_BYTES),
        name=f"post_l{layer}",
    )(*xs, o_fwd, o_bwd, rest, mod, o_norm, ws, bs_t, w_out, norm_ffn, w_ff1, w_ff2, norm_final)


def kernel(x_prompt, x_sample, state_delta, c, c_ctx, w_mod, b_mod, norm_mix, w_in, conv_qkv, a_log,
           dt_bias, o_norm, w_spatial, b_spatial, w_out, norm_ffn, w_ff1, w_ff2, norm_final):
    cond8 = jnp.concatenate([c_ctx[None, :], c, jnp.zeros((8 - 1 - LAT_B, D_MODEL), F32)], axis=0)

    n_qkvg = 4 * D_A
    w_main = jnp.concatenate([w_in[0, :, :n_qkvg], w_in[0, :, n_qkvg + 4 * H_A:]], axis=-1).astype(BF16)
    w_ab = jnp.pad(w_in[0, :, n_qkvg:n_qkvg + 4 * H_A], ((0, 0), (0, LANES - 4 * H_A))).astype(BF16)
    a_row = jnp.pad(a_log.reshape(DEPTH, 1, 2 * H_A), ((0, 0), (0, 0), (2 * H_A, LANES - 4 * H_A)))
    dt_row = jnp.pad(dt_bias.reshape(DEPTH, 1, 2 * H_A), ((0, 0), (0, 0), (2 * H_A, LANES - 4 * H_A)))
    bs_t = jnp.pad(jnp.swapaxes(b_spatial, 1, 2), ((0, 0), (0, 0), (0, LANES - G_B)))
    ws_bf = w_spatial.astype(BF16)
    norm_mix3 = norm_mix.reshape(DEPTH, 1, D_MODEL)
    norm_ffn3 = norm_ffn.reshape(DEPTH, 1, D_MODEL)
    o_norm3 = o_norm.reshape(DEPTH, 1, DV)
    norm_final2 = norm_final.reshape(1, D_MODEL)

    mod = _modulation(cond8, w_mod, b_mod).reshape(DEPTH, 8, 1, N_MOD * D_MODEL)

    xs = (x_prompt.reshape(N_CTX, D_MODEL), x_sample.reshape(N_LAT, D_MODEL))
    new_state = jnp.zeros((CTX_B, DEPTH, 2, H_A, DK, DV), F32)
    for layer in range(DEPTH):
        qkv, rest, gb = _pre_call(layer, xs, mod, norm_mix3, w_main, w_ab, conv_qkv, a_row, dt_row)
        o_fwd, o_bwd, new_state, w_out_bf, w_ff1_bf, w_ff2_bf, w_main, w_ab = _delta_call(
            layer, qkv, gb, state_delta, new_state, w_out, w_ff1, w_ff2, w_in)
        out = _post_call(layer, layer == DEPTH - 1, xs, o_fwd, o_bwd, rest, mod, o_norm3, ws_bf, bs_t,
                         w_out_bf, norm_ffn3, w_ff1_bf, w_ff2_bf, norm_final2)
        xs = (out,)

    y_prompt, y_sample = out
    return (y_prompt.reshape(CTX_B, CTX_T, D_MODEL), y_sample.reshape(LAT_B, LAT_T, D_MODEL), new_state)
```

```python
import functools

import jax
import jax.numpy as jnp
from jax import lax
from jax.experimental import pallas as pl
from jax.experimental.pallas import tpu as pltpu

F32 = jnp.float32
BF16 = jnp.bfloat16

D_MODEL = 1024
DEPTH = 4
N_MOD = 6
H_A = 4
DK = 128
DV = 128
D_A = H_A * DK
G_B = 4
C_B = 128
D_B = G_B * C_B
CHUNK_MLP = 128
D_FF = 4 * D_MODEL
EPS = 1e-6

CTX_B, CTX_T = 32, 256
LAT_B, LAT_T = 2, 4096
N_CTX = CTX_B * CTX_T
N_LAT = LAT_B * LAT_T
N_TOK = N_CTX + N_LAT

LANES = 128
MOD_TN = 3072
TM = 512
N_TILES = N_TOK // TM
CTX_TILES = N_CTX // TM
LAT_TILES_PER_SEQ = LAT_T // TM
HALO = 8
POST_SUB = 256

CHUNK = 128
INV_BASE = 8
BF16_TILE_ROWS = 16
TD = 512
CHUNKS_PER_TD = TD // CHUNK
SEQ_PER_CTX_DTILE = TD // CTX_T
CTX_DTILES = N_CTX // TD
LAT_DTILES_PER_SEQ = LAT_T // TD
N_DTILES = N_TOK // TD

N_FEAT = 6 * D_A
VMEM_LIMIT_BYTES = 60000 * 1024


def _sigmoid(x):
    return 1.0 / (1.0 + jnp.exp(-x))


def _silu(x):
    hx = 0.5 * x
    return hx + hx * jnp.tanh(hx)


_GELU_C0 = 0.7978845608028654
_GELU_C1 = _GELU_C0 * 0.044715


def _gelu_tanh(x):
    hx = 0.5 * x
    return hx + hx * jnp.tanh(x * (_GELU_C0 + _GELU_C1 * (x * x)))


def _softplus(x):
    return jnp.maximum(x, 0.0) + jnp.log1p(jnp.exp(-jnp.abs(x)))


def _dot(a, b):
    return jnp.dot(a, b, preferred_element_type=F32)


def _dot_nt(a, b):
    return lax.dot_general(a, b, (((1,), (1,)), ((), ())), preferred_element_type=F32)


def _dot_tn(a, b):
    return lax.dot_general(a, b, (((0,), (0,)), ((), ())), preferred_element_type=F32)


def _mod_kernel(cond_ref, w_ref, b_ref, o_ref):
    c = cond_ref[...]
    s = (c * _sigmoid(c)).astype(BF16)
    o_ref[0] = _dot(s, w_ref[0].astype(BF16)) + b_ref[0]


def _modulation(cond8, w_mod, b_mod):
    tn = MOD_TN
    return pl.pallas_call(
        _mod_kernel,
        grid=(DEPTH, (N_MOD * D_MODEL) // tn),
        in_specs=[
            pl.BlockSpec((8, D_MODEL), lambda l, j: (0, 0)),
            pl.BlockSpec((1, D_MODEL, tn), lambda l, j: (l, 0, j)),
            pl.BlockSpec((1, 1, tn), lambda l, j: (l, 0, j)),
        ],
        out_specs=pl.BlockSpec((1, 8, tn), lambda l, j: (l, 0, j)),
        out_shape=jax.ShapeDtypeStruct((DEPTH, 8, N_MOD * D_MODEL), F32),
        compiler_params=pltpu.CompilerParams(
            dimension_semantics=("arbitrary", "arbitrary"), vmem_limit_bytes=VMEM_LIMIT_BYTES),
        name="modulation",
    )(cond8, w_mod, b_mod.reshape(DEPTH, 1, N_MOD * D_MODEL))


def _mod_row(i):
    return jnp.where(i < CTX_TILES, 0, 1 + (i - CTX_TILES) // LAT_TILES_PER_SEQ)


def _pre_kernel(split, *refs):
    i = pl.program_id(0)
    is_ctx = i < CTX_TILES
    if split:
        (xc_ref, xl_ref, xcp_ref, xlp_ref, xcn_ref, xln_ref, mod_ref, nw_ref, w_ref, wab_ref,
         conv_ref, arow_ref, dtrow_ref, qkv_ref, rest_ref, gb_ref) = refs
        x = jnp.where(is_ctx, xc_ref[...], xl_ref[...])
        x_prev = jnp.where(is_ctx, xcp_ref[...], xlp_ref[...])
        x_next = jnp.where(is_ctx, xcn_ref[...], xln_ref[...])
    else:
        (x_ref, xp_ref, xn_ref, mod_ref, nw_ref, w_ref, wab_ref,
         conv_ref, arow_ref, dtrow_ref, qkv_ref, rest_ref, gb_ref) = refs
        x, x_prev, x_next = x_ref[...], xp_ref[...], xn_ref[...]
    m = mod_ref[0]
    sh1 = m[:, 0:D_MODEL]
    coef = nw_ref[...] * (1.0 + m[:, D_MODEL:2 * D_MODEL])

    def norm_mod(xx):
        ms = jnp.mean(xx * xx, axis=-1, keepdims=True)
        return ((xx * lax.rsqrt(ms + EPS)) * coef + sh1).astype(BF16)

    h = norm_mod(x)
    hh = norm_mod(jnp.concatenate([x_prev, x_next], axis=0))
    h_ext = jnp.concatenate([h, hh], axis=0)

    lat_j = (i - CTX_TILES) % LAT_TILES_PER_SEQ
    starts_seq = jnp.logical_or(is_ctx, lat_j == 0)
    ends_seq = jnp.logical_or(is_ctx, lat_j == LAT_TILES_PER_SEQ - 1)
    r8 = lax.broadcasted_iota(jnp.int32, (HALO, 1), 0)

    def shifted(p, prev_row, next_row):
        dn = pltpu.roll(p, 1, 0)
        up = pltpu.roll(p, TM - 1, 0)
        dn_parts, up_parts = [], []
        for b in range(0, TM, CTX_T):
            slab = dn[b:b + HALO]
            if b == 0:
                slab = jnp.where(r8 == 0, jnp.where(starts_seq, 0.0, prev_row), slab)
            else:
                slab = jnp.where(jnp.logical_and(r8 == 0, is_ctx), 0.0, slab)
            dn_parts += [slab, dn[b + HALO:b + CTX_T]]
            e = b + CTX_T
            slab = up[e - HALO:e]
            if e == TM:
                slab = jnp.where(r8 == HALO - 1, jnp.where(ends_seq, 0.0, next_row), slab)
            else:
                slab = jnp.where(jnp.logical_and(r8 == HALO - 1, is_ctx), 0.0, slab)
            up_parts += [up[b:e - HALO], slab]
        return jnp.concatenate(dn_parts, axis=0), jnp.concatenate(up_parts, axis=0)

    cw = conv_ref[...]
    for blk in range(3):
        cols = slice(blk * D_A, (blk + 1) * D_A)
        pe = _dot(h_ext, w_ref[:, cols])
        rcols = slice((3 + blk) * D_A, (4 + blk) * D_A)
        rest_ref[:, blk * D_A:(blk + 1) * D_A] = _dot(h, w_ref[:, rcols]).astype(BF16)
        if blk == 0:
            ab = _dot(h, wab_ref[...])
            lane = lax.broadcasted_iota(jnp.int32, (TM, LANES), 1)
            g_log = -jnp.exp(arow_ref[...]) * _softplus(ab + dtrow_ref[...])
            gb_ref[...] = jnp.where(lane < 2 * H_A, _sigmoid(ab), g_log)
        p = pe[0:TM]
        xm1, xp1 = shifted(p, pe[TM + HALO - 1:TM + HALO], pe[TM + HALO:TM + HALO + 1])
        y = _silu(cw[0:1, cols] * xm1 + cw[1:2, cols] * p + cw[2:3, cols] * xp1)
        if blk < 2:
            scale = DK ** -0.5 if blk == 0 else 1.0
            for hd in range(H_A):
                yh = y[:, hd * DK:(hd + 1) * DK]
                inv_norm = lax.rsqrt(jnp.sum(yh * yh, axis=-1, keepdims=True) + EPS) * scale
                qkv_ref[:, blk * D_A + hd * DK:blk * D_A + (hd + 1) * DK] = (yh * inv_norm).astype(BF16)
        else:
            qkv_ref[:, cols] = y.astype(BF16)


def _pre_call(layer, xs, mod, norm_mix, w_main, w_ab, conv_qkv, a_row, dt_row):
    split = len(xs) == 2
    bpt = TM // HALO
    const = dict(pipeline_mode=pl.Buffered(1))

    def tile_specs(first_tile, n_tiles):
        def local(i):
            return jnp.clip(i - first_tile, 0, n_tiles - 1)
        return (
            pl.BlockSpec((TM, D_MODEL), lambda i: (local(i), 0)),
            pl.BlockSpec((HALO, D_MODEL), lambda i: (jnp.maximum(local(i) * bpt - 1, 0), 0)),
            pl.BlockSpec((HALO, D_MODEL),
                         lambda i: (jnp.minimum((local(i) + 1) * bpt, n_tiles * bpt - 1), 0)),
        )

    if split:
        c_specs = tile_specs(0, CTX_TILES)
        l_specs = tile_specs(CTX_TILES, N_TILES - CTX_TILES)
        x_specs = [c_specs[0], l_specs[0], c_specs[1], l_specs[1], c_specs[2], l_specs[2]]
        x_args = [xs[0], xs[1]] * 3
    else:
        x_specs = list(tile_specs(0, N_TILES))
        x_args = [xs[0]] * 3
    return pl.pallas_call(
        functools.partial(_pre_kernel, split),
        grid=(N_TILES,),
        in_specs=x_specs + [
            pl.BlockSpec((None, 1, 1, N_MOD * D_MODEL), lambda i: (layer, _mod_row(i), 0, 0)),
            pl.BlockSpec((None, 1, D_MODEL), lambda i: (layer, 0, 0), **const),
            pl.BlockSpec((None, D_MODEL, N_FEAT), lambda i: (layer, 0, 0), **const),
            pl.BlockSpec((None, D_MODEL, LANES), lambda i: (layer, 0, 0), **const),
            pl.BlockSpec((None, 3, 3 * D_A), lambda i: (layer, 0, 0), **const),
            pl.BlockSpec((None, 1, LANES), lambda i: (layer, 0, 0), **const),
            pl.BlockSpec((None, 1, LANES), lambda i: (layer, 0, 0), **const),
        ],
        out_specs=[
            pl.BlockSpec((TM, 3 * D_A), lambda i: (i, 0)),
            pl.BlockSpec((TM, 3 * D_A), lambda i: (i, 0)),
            pl.BlockSpec((TM, LANES), lambda i: (i, 0)),
        ],
        out_shape=[
            jax.ShapeDtypeStruct((N_TOK, 3 * D_A), BF16),
            jax.ShapeDtypeStruct((N_TOK, 3 * D_A), BF16),
            jax.ShapeDtypeStruct((N_TOK, LANES), F32),
        ],
        compiler_params=pltpu.CompilerParams(
            dimension_semantics=("arbitrary",), vmem_limit_bytes=VMEM_LIMIT_BYTES),
        name=f"pre_l{layer}",
    )(*x_args, mod, norm_mix, w_main, w_ab, conv_qkv, a_row, dt_row)


def _delta_tile_row(rev, t):
    if not rev:
        return t
    lat = t - CTX_DTILES
    b = lat // LAT_DTILES_PER_SEQ
    j = LAT_DTILES_PER_SEQ - 1 - lat % LAT_DTILES_PER_SEQ
    return jnp.where(t < CTX_DTILES, t, CTX_DTILES + b * LAT_DTILES_PER_SEQ + j)


def _mm_inv(a, b):
    return _dot(a.astype(BF16), b.astype(BF16))


def _row_groups(x, size):
    return [x[g * size:(g + 1) * size] for g in range(x.shape[0] // size)]


def _delta_direction(rev, is_ctx, q_ref, k_ref, v_ref, gb_ref, o_ref, state0):
    row = lax.broadcasted_iota(jnp.int32, (CHUNK, CHUNK), 0)
    col = lax.broadcasted_iota(jnp.int32, (CHUNK, CHUNK), 1)
    incl = row <= col if rev else row >= col
    strict = row < col if rev else row > col
    eye = (row == col).astype(F32)
    base_mask = (row // INV_BASE) == (col // INV_BASE)
    levels = []
    blk = INV_BASE
    while blk < CHUNK:
        levels.append((blk, jnp.logical_and((row // (2 * blk)) == (col // (2 * blk)),
                                            (row // blk) != (col // blk))))
        blk *= 2
    rix = lax.broadcasted_iota(jnp.int32, (CHUNK, LANES), 0)
    lane_beta = H_A if rev else 0
    lane_g = 3 * H_A if rev else 2 * H_A
    last = 0 if rev else CHUNK - 1

    chunk_order = range(CHUNKS_PER_TD - 1, -1, -1) if rev else range(CHUNKS_PER_TD)
    chunk_rows = [slice(cc * CHUNK, (cc + 1) * CHUNK) for cc in chunk_order]
    probs = [[dict(hd=hd, hcols=slice(hd * DK, (hd + 1) * DK)) for hd in range(H_A)]
             for _ in range(CHUNKS_PER_TD)]
    carry = dict(state=list(state0), mid=None)

    def prep(c):
        rows = chunk_rows[c]
        gb = gb_ref[rows, :]
        fwd_sum = gb
        s = 1
        while s < CHUNK:
            fwd_sum = fwd_sum + jnp.where(rix >= s, pltpu.roll(fwd_sum, s, 0), 0.0)
            s *= 2
        cum = fwd_sum[CHUNK - 1:CHUNK, :] - fwd_sum + gb if rev else fwd_sum
        cum_t = cum.T
        for p in probs[c]:
            hd = p["hd"]
            g_b = jnp.broadcast_to(cum[:, lane_g + hd:lane_g + hd + 1], (CHUNK, CHUNK))
            p.update(beta_b=jnp.broadcast_to(gb[:, lane_beta + hd:lane_beta + hd + 1], (CHUNK, CHUNK)),
                     g_b=g_b,
                     g_row=cum_t[lane_g + hd:lane_g + hd + 1, :],
                     g_tot=g_b[last:last + 1, :])

    def kq_stage(c):
        rows = chunk_rows[c]
        for p in probs[c]:
            k_bf = k_ref[rows, p["hcols"]]
            kb = (k_bf.astype(F32) * p["beta_b"]).astype(BF16)
            kq = _dot_nt(jnp.concatenate([kb, q_ref[rows, p["hcols"]]], axis=0), k_bf)
            decay = jnp.exp(jnp.minimum(p["g_b"] - p["g_row"], 0.0))
            p["a"] = jnp.where(strict, kq[0:CHUNK] * decay, 0.0)
            p["qk"] = jnp.where(incl, kq[CHUNK:2 * CHUNK] * decay, 0.0).astype(BF16)

    def base_first(c):
        for p in probs[c]:
            a0 = jnp.where(base_mask, p["a"], 0.0)
            p["inv"] = eye - a0
            p["pw"] = _mm_inv(a0, a0).astype(BF16)

    def base_mid(c):
        for p in probs[c]:
            st = _dot(jnp.concatenate([p["pw"], p["inv"].astype(BF16)], axis=0), p["pw"])
            p["inv"] = p["inv"] + st[CHUNK:2 * CHUNK]
            p["pw"] = st[0:CHUNK].astype(BF16)

    def base_last(c):
        for p in probs[c]:
            p["inv"] = p["inv"] + _dot(p["inv"].astype(BF16), p["pw"])

    upd_parity = 0 if rev else 1

    def merge_first(size, off_mask, c):
        for p in probs[c]:
            p["groups"] = _row_groups(p["inv"], size)
            lhs = jnp.concatenate(p["groups"][upd_parity::2], axis=0)
            p["tmp"] = _mm_inv(lhs, jnp.where(off_mask, p["a"], 0.0))

    def merge_second(size, c):
        for p in probs[c]:
            upd = _row_groups(_mm_inv(p["tmp"], p["inv"]), size)
            new = [g - upd[i // 2] if i % 2 == upd_parity else g for i, g in enumerate(p["groups"])]
            p["inv"] = jnp.concatenate(new, axis=0)

    def to_bf16(c):
        for p in probs[c]:
            p["inv"] = p["inv"].astype(BF16)
            p["a"] = p["a"].astype(BF16)

    def merge_first_bf16(size, mask_bf, c):
        for p in probs[c]:
            p["groups"] = _row_groups(p["inv"], size)
            lhs = jnp.concatenate(p["groups"][upd_parity::2], axis=0)
            p["tmp"] = _dot(lhs, p["a"] * mask_bf).astype(BF16)

    def merge_second_bf16(size, c):
        for p in probs[c]:
            upd = _row_groups(_dot(p["tmp"], p["inv"]), size)
            new = [(g.astype(F32) - upd[i // 2]).astype(BF16) if i % 2 == upd_parity else g
                   for i, g in enumerate(p["groups"])]
            p["inv"] = jnp.concatenate(new, axis=0)

    def solve(c):
        rows = chunk_rows[c]
        for p in probs[c]:
            beta_b = p["beta_b"]
            rhs = jnp.concatenate(
                [v_ref[rows, p["hcols"]].astype(F32) * beta_b,
                 k_ref[rows, p["hcols"]].astype(F32) * beta_b * jnp.exp(p["g_b"])], axis=-1)
            sol = _dot(p["inv"].astype(BF16), rhs.astype(BF16))
            p["u"] = sol[:, 0:DV]
            p["w"] = sol[:, DV:2 * DV].astype(BF16)

    def rec_first(c):
        if c == CHUNKS_PER_TD // 2:
            carry["mid"] = carry["state"]
            carry["state"] = [jnp.where(is_ctx, 0.0, st) for st in carry["state"]]
        for p in probs[c]:
            p["st_bf"] = carry["state"][p["hd"]].astype(BF16)
            p["ws"] = _dot(p["w"], p["st_bf"])

    def rec_second(c):
        state = list(carry["state"])
        rows = chunk_rows[c]
        for p in probs[c]:
            g_b, g_tot = p["g_b"], p["g_tot"]
            q_dec = (q_ref[rows, p["hcols"]].astype(F32) * jnp.exp(g_b)).astype(BF16)
            k_dec = (k_ref[rows, p["hcols"]].astype(F32) * jnp.exp(g_tot - g_b)).astype(BF16)
            v_new = (p["u"] - p["ws"]).astype(BF16)
            o_ref[rows, p["hcols"]] = _dot(jnp.concatenate([q_dec, p["qk"]], axis=1),
                                           jnp.concatenate([p["st_bf"], v_new], axis=0)).astype(BF16)
            state[p["hd"]] = state[p["hd"]] * jnp.exp(g_tot) + _dot_tn(k_dec, v_new)
        carry["state"] = state

    stages = [prep, kq_stage, base_first]
    n = 2
    while 2 * n < INV_BASE:
        stages.append(base_mid)
        n *= 2
    stages.append(base_last)
    in_bf16 = False
    for size, off_mask in levels:
        if size % BF16_TILE_ROWS == 0 and not in_bf16:
            stages.append(to_bf16)
            in_bf16 = True
        if in_bf16:
            mask_bf = jnp.where(off_mask, 1.0, 0.0).astype(BF16)
            stages.append(functools.partial(merge_first_bf16, size, mask_bf))
            stages.append(functools.partial(merge_second_bf16, size))
        else:
            stages.append(functools.partial(merge_first, size, off_mask))
            stages.append(functools.partial(merge_second, size))
    stages.append(solve)
    return stages, rec_first, rec_second, carry


def _delta_kernel(qf_ref, kf_ref, vf_ref, gbf_ref, qb_ref, kb_ref, vb_ref, gbb_ref, s0_ref,
                  wout_ref, wff1_ref, wff2_ref, ns_in_ref,
                  of_ref, ob_ref, ns_ref, wout_bf_ref, wff1_bf_ref, wff2_bf_ref, s_scr):
    del ns_in_ref
    wout_bf_ref[...] = wout_ref[...].astype(BF16)
    wff1_bf_ref[...] = wff1_ref[...].astype(BF16)
    wff2_bf_ref[...] = wff2_ref[...].astype(BF16)
    t = pl.program_id(0)
    is_ctx = t < CTX_DTILES

    @pl.when(is_ctx)
    def _():
        s_scr[...] = jnp.zeros_like(s_scr)

    @pl.when(jnp.logical_and(t >= CTX_DTILES, (t - CTX_DTILES) % LAT_DTILES_PER_SEQ == 0))
    def _():
        s_scr[...] = s0_ref[...]

    dirs = [
        _delta_direction(False, is_ctx, qf_ref, kf_ref, vf_ref, gbf_ref, of_ref,
                         [s_scr[0, hd] for hd in range(H_A)]),
        _delta_direction(True, is_ctx, qb_ref, kb_ref, vb_ref, gbb_ref, ob_ref,
                         [s_scr[1, hd] for hd in range(H_A)]),
    ]
    for k in range(len(dirs[0][0])):
        for stages, _, _, _ in dirs:
            for c in range(CHUNKS_PER_TD):
                stages[k](c)
    for c in range(CHUNKS_PER_TD):
        for _, rec_first, _, _ in dirs:
            rec_first(c)
        for _, _, rec_second, _ in dirs:
            rec_second(c)

    for d, (_, _, _, carry) in enumerate(dirs):
        for hd in range(H_A):
            s_scr[d, hd] = carry["state"][hd]

    @pl.when(is_ctx)
    def _():
        for d, (_, _, _, carry) in enumerate(dirs):
            first, second = (1, 0) if d == 1 else (0, 1)
            for hd in range(H_A):
                ns_ref[first, d, hd] = carry["mid"][hd]
                ns_ref[second, d, hd] = carry["state"][hd]


def _delta_call(layer, qkv, gb, state_delta, new_state, w_out, w_ff1, w_ff2):
    wo_rows, f1_cols, f2_rows = D_MODEL // N_DTILES, D_FF // N_DTILES, D_FF // N_DTILES

    def tile_specs(rev):
        def spec(width, col_block):
            return pl.BlockSpec((TD, width), lambda t: (_delta_tile_row(rev, t), col_block))
        return [spec(D_A, 0), spec(D_A, 1), spec(D_A, 2), spec(LANES, 0)]

    def s0_index(t):
        b = jnp.maximum(t - CTX_DTILES, 0) // LAT_DTILES_PER_SEQ
        return (b, layer, 0, 0, 0, 0)

    def ns_index(t):
        return (jnp.minimum(t, CTX_DTILES - 1), layer, 0, 0, 0, 0)

    return pl.pallas_call(
        _delta_kernel,
        grid=(N_DTILES,),
        in_specs=tile_specs(False) + tile_specs(True) + [
            pl.BlockSpec((None, None, 2, H_A, DK, DV), s0_index),
            pl.BlockSpec((None, wo_rows, D_MODEL), lambda t: (layer, t, 0)),
            pl.BlockSpec((None, D_MODEL, f1_cols), lambda t: (layer, 0, t)),
            pl.BlockSpec((None, f2_rows, D_MODEL), lambda t: (layer, t, 0)),
            pl.BlockSpec(memory_space=pl.ANY),
        ],
        out_specs=[
            pl.BlockSpec((TD, D_A), lambda t: (_delta_tile_row(False, t), 0)),
            pl.BlockSpec((TD, D_A), lambda t: (_delta_tile_row(True, t), 0)),
            pl.BlockSpec((SEQ_PER_CTX_DTILE, None, 2, H_A, DK, DV), ns_index),
            pl.BlockSpec((wo_rows, D_MODEL), lambda t: (t, 0)),
            pl.BlockSpec((D_MODEL, f1_cols), lambda t: (0, t)),
            pl.BlockSpec((f2_rows, D_MODEL), lambda t: (t, 0)),
        ],
        out_shape=[
            jax.ShapeDtypeStruct((N_TOK, D_A), BF16),
            jax.ShapeDtypeStruct((N_TOK, D_A), BF16),
            jax.ShapeDtypeStruct(new_state.shape, F32),
            jax.ShapeDtypeStruct((D_MODEL, D_MODEL), BF16),
            jax.ShapeDtypeStruct((D_MODEL, D_FF), BF16),
            jax.ShapeDtypeStruct((D_FF, D_MODEL), BF16),
        ],
        scratch_shapes=[pltpu.VMEM((2, H_A, DK, DV), F32)],
        input_output_aliases={12: 2},
        compiler_params=pltpu.CompilerParams(
            dimension_semantics=("arbitrary",), vmem_limit_bytes=VMEM_LIMIT_BYTES),
        name=f"delta_l{layer}",
    )(qkv, qkv, qkv, gb, qkv, qkv, qkv, gb, state_delta, w_out, w_ff1, w_ff2, new_state)


def _post_kernel(final, split, *refs):
    is_ctx = pl.program_id(0) < CTX_TILES
    (of_ref, ob_ref, rest_ref, mod_ref, onorm_ref, ws_ref, bs_ref, wout_ref, nffn_ref, wff1_ref,
     wff2_ref, nfin_ref) = refs[1 + split:13 + split]
    out_refs = refs[13 + split:]
    m = mod_ref[0]
    ga1 = m[:, 2 * D_MODEL:3 * D_MODEL]
    sh2 = m[:, 3 * D_MODEL:4 * D_MODEL]
    coef2 = nffn_ref[...] * (1.0 + m[:, 4 * D_MODEL:5 * D_MODEL])
    ga2 = m[:, 5 * D_MODEL:6 * D_MODEL]
    onw = onorm_ref[...]
    bs = bs_ref[...]

    def head(rows):
        if split:
            x = jnp.where(is_ctx, refs[0][rows, :], refs[1][rows, :])
        else:
            x = refs[0][rows, :]
        o = of_ref[rows, :].astype(F32) + ob_ref[rows, :].astype(F32)
        mix_in = []
        for hd in range(H_A):
            oh = o[:, hd * DV:(hd + 1) * DV]
            oh = oh * (lax.rsqrt(jnp.mean(oh * oh, axis=-1, keepdims=True) + EPS) * onw)
            gate = _silu(rest_ref[rows, hd * DV:(hd + 1) * DV].astype(F32))
            mix_in.append((oh * gate).astype(BF16))
        for g in range(G_B):
            ug = _gelu_tanh(rest_ref[rows, D_A + g * C_B:D_A + (g + 1) * C_B].astype(F32))
            vg = _gelu_tanh(rest_ref[rows, 2 * D_A + g * C_B:2 * D_A + (g + 1) * C_B].astype(F32))
            dv = vg - jnp.mean(vg, axis=-1, keepdims=True)
            var = jnp.mean(dv * dv, axis=-1, keepdims=True)
            vgn = (dv * lax.rsqrt(var + EPS)).astype(BF16)
            parts = []
            for c in range(POST_SUB // CHUNK_MLP):
                crows = slice(c * CHUNK_MLP, (c + 1) * CHUNK_MLP)
                mixed = _dot(ws_ref[g], vgn[crows]) + bs[:, g:g + 1]
                parts.append((ug[crows] * mixed).astype(BF16))
            mix_in.append(jnp.concatenate(parts, axis=0))
        mix = jnp.concatenate(mix_in, axis=-1)
        x1 = x + ga1 * _dot(mix, wout_ref[...])
        ms = jnp.mean(x1 * x1, axis=-1, keepdims=True)
        return x1, ((x1 * lax.rsqrt(ms + EPS)) * coef2 + sh2).astype(BF16)

    subs = [slice(r0, r0 + POST_SUB) for r0 in range(0, TM, POST_SUB)]
    ys = []
    nxt = head(subs[0])
    for s, rows in enumerate(subs):
        x1, h2 = nxt
        a = jnp.maximum(_dot(h2, wff1_ref[...]), 0.0)
        if s + 1 < len(subs):
            nxt = head(subs[s + 1])
        x2 = x1 + ga2 * _dot((a * a).astype(BF16), wff2_ref[...])
        if final:
            ms2 = jnp.mean(x2 * x2, axis=-1, keepdims=True)
            x2 = (x2 * lax.rsqrt(ms2 + EPS)) * nfin_ref[...]
        ys.append((rows, x2))

    if not final:
        for rows, y in ys:
            out_refs[0][rows, :] = y
        return
    yc_ref, yl_ref = out_refs

    @pl.when(is_ctx)
    def _():
        for rows, y in ys:
            yc_ref[rows, :] = y

    @pl.when(jnp.logical_not(is_ctx))
    def _():
        for rows, y in ys:
            yl_ref[rows, :] = y


def _post_call(layer, final, xs, o_fwd, o_bwd, rest, mod, o_norm, ws, bs_t, w_out, norm_ffn, w_ff1,
               w_ff2, norm_final):
    split = len(xs) == 2
    const = dict(pipeline_mode=pl.Buffered(1))
    if split:
        x_specs = [pl.BlockSpec((TM, D_MODEL), lambda i: (jnp.minimum(i, CTX_TILES - 1), 0)),
                   pl.BlockSpec((TM, D_MODEL), lambda i: (jnp.maximum(i - CTX_TILES, 0), 0))]
    else:
        x_specs = [pl.BlockSpec((TM, D_MODEL), lambda i: (i, 0))]
    if final:
        out_specs = [
            pl.BlockSpec((TM, D_MODEL), lambda i: (jnp.minimum(i, CTX_TILES - 1), 0)),
            pl.BlockSpec((TM, D_MODEL), lambda i: (jnp.maximum(i - CTX_TILES, 0), 0)),
        ]
        out_shape = [jax.ShapeDtypeStruct((N_CTX, D_MODEL), F32),
                     jax.ShapeDtypeStruct((N_LAT, D_MODEL), F32)]
    else:
        out_specs = pl.BlockSpec((TM, D_MODEL), lambda i: (i, 0))
        out_shape = jax.ShapeDtypeStruct((N_TOK, D_MODEL), F32)
    return pl.pallas_call(
        functools.partial(_post_kernel, final, split),
        grid=(N_TILES,),
        in_specs=x_specs + [
            pl.BlockSpec((TM, D_A), lambda i: (i, 0)),
            pl.BlockSpec((TM, D_A), lambda i: (i, 0)),
            pl.BlockSpec((TM, 3 * D_A), lambda i: (i, 0)),
            pl.BlockSpec((None, 1, 1, N_MOD * D_MODEL), lambda i: (layer, _mod_row(i), 0, 0)),
            pl.BlockSpec((None, 1, DV), lambda i: (layer, 0, 0), **const),
            pl.BlockSpec((None, G_B, CHUNK_MLP, CHUNK_MLP), lambda i: (layer, 0, 0, 0), **const),
            pl.BlockSpec((None, CHUNK_MLP, LANES), lambda i: (layer, 0, 0), **const),
            pl.BlockSpec((D_MODEL, D_MODEL), lambda i: (0, 0), **const),
            pl.BlockSpec((None, 1, D_MODEL), lambda i: (layer, 0, 0), **const),
            pl.BlockSpec((D_MODEL, D_FF), lambda i: (0, 0), **const),
            pl.BlockSpec((D_FF, D_MODEL), lambda i: (0, 0), **const),
            pl.BlockSpec((1, D_MODEL), lambda i: (0, 0), **const),
        ],
        out_specs=out_specs,
        out_shape=out_shape,
        compiler_params=pltpu.CompilerParams(
            dimension_semantics=("arbitrary",), vmem_limit_bytes=VMEM_LIMIT_BYTES),
        name=f"post_l{layer}",
    )(*xs, o_fwd, o_bwd, rest, mod, o_norm, ws, bs_t, w_out, norm_ffn, w_ff1, w_ff2, norm_final)


def kernel(x_prompt, x_sample, state_delta, c, c_ctx, w_mod, b_mod, norm_mix, w_in, conv_qkv, a_log,
           dt_bias, o_norm, w_spatial, b_spatial, w_out, norm_ffn, w_ff1, w_ff2, norm_final):
    cond8 = jnp.concatenate([c_ctx[None, :], c, jnp.zeros((8 - 1 - LAT_B, D_MODEL), F32)], axis=0)

    n_qkvg = 4 * D_A
    w_main = jnp.concatenate([w_in[:, :, :n_qkvg], w_in[:, :, n_qkvg + 4 * H_A:]], axis=-1).astype(BF16)
    w_ab = jnp.pad(w_in[:, :, n_qkvg:n_qkvg + 4 * H_A], ((0, 0), (0, 0), (0, LANES - 4 * H_A))).astype(BF16)
    a_row = jnp.pad(a_log.reshape(DEPTH, 1, 2 * H_A), ((0, 0), (0, 0), (2 * H_A, LANES - 4 * H_A)))
    dt_row = jnp.pad(dt_bias.reshape(DEPTH, 1, 2 * H_A), ((0, 0), (0, 0), (2 * H_A, LANES - 4 * H_A)))
    bs_t = jnp.pad(jnp.swapaxes(b_spatial, 1, 2), ((0, 0), (0, 0), (0, LANES - G_B)))
    ws_bf = w_spatial.astype(BF16)
    norm_mix3 = norm_mix.reshape(DEPTH, 1, D_MODEL)
    norm_ffn3 = norm_ffn.reshape(DEPTH, 1, D_MODEL)
    o_norm3 = o_norm.reshape(DEPTH, 1, DV)
    norm_final2 = norm_final.reshape(1, D_MODEL)

    mod = _modulation(cond8, w_mod, b_mod).reshape(DEPTH, 8, 1, N_MOD * D_MODEL)

    xs = (x_prompt.reshape(N_CTX, D_MODEL), x_sample.reshape(N_LAT, D_MODEL))
    new_state = jnp.zeros((CTX_B, DEPTH, 2, H_A, DK, DV), F32)
    for layer in range(DEPTH):
        qkv, rest, gb = _pre_call(layer, xs, mod, norm_mix3, w_main, w_ab, conv_qkv, a_row, dt_row)
        o_fwd, o_bwd, new_state, w_out_bf, w_ff1_bf, w_ff2_bf = _delta_call(
            layer, qkv, gb, state_delta, new_state, w_out, w_ff1, w_ff2)
        out = _post_call(layer, layer == DEPTH - 1, xs, o_fwd, o_bwd, rest, mod, o_norm3, ws_bf, bs_t,
                         w_out_bf, norm_ffn3, w_ff1_bf, w_ff2_bf, norm_final2)
        xs = (out,)

    y_prompt, y_sample = out
    return (y_prompt.reshape(CTX_B, CTX_T, D_MODEL), y_sample.reshape(LAT_B, LAT_T, D_MODEL), new_state)
```
